```python
import jax, jax.numpy as jnp
from jax import lax
import numpy as np

D_MODEL = 1024
BATCH = 2
SEQ = 8192
DEPTH = 1

GRID_W = 64
CTX_LEN = 256
EPS = 1e-6
N_MOD = 6
F_GROUPS = 8
F_GROUP_DIM = 128
F_WIDTH = F_GROUPS * F_GROUP_DIM
SSD_HEADS = 32
SSD_HEAD_DIM = 64
SSD_INNER = SSD_HEADS * SSD_HEAD_DIM
SSD_GROUPS = 4
SSD_HPG = SSD_HEADS // SSD_GROUPS
SSD_STATE = 128
SSD_BC = SSD_GROUPS * SSD_STATE
SSD_CONV = 3
SSD_CHUNK = 128
XBC_WIDTH = SSD_INNER + 2 * SSD_BC
D_FF = 2816
FFN_CONV = 3
IN_WIDTHS = (F_WIDTH, XBC_WIDTH, SSD_INNER, 2 * SSD_HEADS, D_MODEL, D_MODEL)
IN_TOTAL = F_WIDTH + XBC_WIDTH + SSD_INNER + 2 * SSD_HEADS + 2 * D_MODEL

kernel_name = "hybrid_fnet_ssd_convffn_dit_block"


def _rmsnorm(x, g):
    x32 = x.astype(jnp.float32)
    y = x32 * lax.rsqrt(jnp.mean(x32 * x32, axis=-1, keepdims=True) + EPS)
    return (y * g.astype(jnp.float32)).astype(x.dtype)


def _adaln(cond, w_mod, b_mod):
    m = jax.nn.silu(cond) @ w_mod + b_mod
    m = m.reshape(cond.shape[:-1] + (N_MOD, D_MODEL))
    return [m[..., k:k + 1, :] for k in range(N_MOD)]


def _modulate(x, shift, scale):
    return x * (1.0 + scale) + shift


def _split_in(proj):
    idx = np.cumsum(IN_WIDTHS)[:-1].tolist()
    return jnp.split(proj, idx, axis=-1)


def _dwconv_seq(u, w, b):
    k = w.shape[0]
    out = lax.conv_general_dilated(
        u, w[:, None, :].astype(u.dtype), window_strides=(1,), padding=[(k // 2, k // 2)],
        dimension_numbers=("NWC", "WIO", "NWC"), feature_group_count=u.shape[-1])
    return out + b.astype(u.dtype)


def _dwconv_grid(u, w, b):
    bsz, length, ch = u.shape
    rows = length // GRID_W
    k = w.shape[0]
    img = u.reshape(bsz, rows, GRID_W, ch)
    out = lax.conv_general_dilated(
        img, w[:, :, None, :].astype(u.dtype), window_strides=(1, 1),
        padding=[(k // 2, k // 2), (k // 2, k // 2)],
        dimension_numbers=("NHWC", "HWIO", "NHWC"), feature_group_count=ch)
    return out.reshape(bsz, length, ch) + b.astype(u.dtype)


def _segsum(a):
    t = a.shape[-1]
    rep = jnp.broadcast_to(a[..., :, None], a.shape + (t,))
    strict = jnp.tril(jnp.ones((t, t), dtype=bool), -1)
    ss = jnp.cumsum(jnp.where(strict, rep, 0.0), axis=-2)
    incl = jnp.tril(jnp.ones((t, t), dtype=bool))
    return jnp.where(incl, ss, -jnp.inf)


def _ssd_chunked(xh, dt, a, bm, cm, h0):
    bsz, length, g, hg, p = xh.shape
    n = bm.shape[-1]
    q = SSD_CHUNK
    nc = length // q
    xd = (xh * dt[..., None]).reshape(bsz, nc, q, g, hg, p)
    la = (dt * a).reshape(bsz, nc, q, g, hg).transpose(0, 3, 4, 1, 2)
    bc = bm.reshape(bsz, nc, q, g, n)
    cc = cm.reshape(bsz, nc, q, g, n)
    la_cum = jnp.cumsum(la, axis=-1)
    lmat = jnp.exp(_segsum(la))
    scores = jnp.einsum("bclgn,bcsgn->bgcls", cc, bc)
    y_diag = jnp.einsum("bgcls,bghcls,bcsghp->bclghp", scores, lmat, xd)
    decay_states = jnp.exp(la_cum[..., -1:] - la_cum)
    states = jnp.einsum("bcsgn,bghcs,bcsghp->bcghpn", bc, decay_states, xd)
    states = jnp.concatenate([h0[:, None], states], axis=1)
    chunk_a = jnp.pad(la_cum[..., -1], ((0, 0), (0, 0), (0, 0), (1, 0)))
    decay_chunk = jnp.exp(_segsum(chunk_a))
    new_states = jnp.einsum("bghzc,bcghpn->bzghpn", decay_chunk, states)
    enter, final = new_states[:, :-1], new_states[:, -1]
    y_off = jnp.einsum("bclgn,bcghpn,bghcl->bclghp", cc, enter, jnp.exp(la_cum))
    y = (y_diag + y_off).reshape(bsz, length, g, hg, p)
    return y, final


def _ssd_prep(xbc, dt_raw, conv_w, conv_b, dt_bias):
    bsz, length = xbc.shape[:2]
    xbc = jax.nn.silu(_dwconv_seq(xbc, conv_w, conv_b)).astype(jnp.float32)
    xs, bm, cm = jnp.split(xbc, [SSD_INNER, SSD_INNER + SSD_BC], axis=-1)
    xh = xs.reshape(bsz, length, SSD_GROUPS, SSD_HPG, SSD_HEAD_DIM)
    bm = bm.reshape(bsz, length, SSD_GROUPS, SSD_STATE)
    cm = cm.reshape(bsz, length, SSD_GROUPS, SSD_STATE)
    dt = jax.nn.softplus(dt_raw.astype(jnp.float32).reshape(bsz, length, 2, SSD_GROUPS, SSD_HPG)
                         + dt_bias.astype(jnp.float32).reshape(2, SSD_GROUPS, SSD_HPG))
    return xh, bm, cm, dt


def _ssd_bidir(xh, bm, cm, dt, a, h0_f, h0_b):
    y_f, s_f = _ssd_chunked(xh, dt[:, :, 0], a[0], bm, cm, h0_f)
    flip = lambda t: jnp.flip(t, axis=1)
    y_b, s_b = _ssd_chunked(flip(xh), flip(dt[:, :, 1]), a[1], flip(bm), flip(cm), h0_b)
    return y_f + flip(y_b), s_f, s_b


def _mixer_merge(u_f, y, xh, z, g_f, g_s, d_skip, ssd_norm_g, w_fa, w_sb, w_o):
    bsz, length = u_f.shape[:2]
    dtype = u_f.dtype
    uf = u_f.astype(jnp.float32).reshape(bsz, length, F_GROUPS, F_GROUP_DIM)
    mixed = jnp.fft.fft2(uf, axes=(1, 3), norm="ortho").real
    branch_f = mixed.reshape(bsz, length, F_WIDTH).astype(dtype) @ w_fa
    ys = y + d_skip.astype(jnp.float32).reshape(SSD_GROUPS, SSD_HPG)[:, :, None] * xh
    ys = ys.reshape(bsz, length, SSD_GROUPS, SSD_HPG * SSD_HEAD_DIM)
    ys = ys * jax.nn.silu(z.astype(jnp.float32)).reshape(bsz, length, SSD_GROUPS, SSD_HPG * SSD_HEAD_DIM)
    ys = ys * lax.rsqrt(jnp.mean(ys * ys, axis=-1, keepdims=True) + EPS)
    ys = (ys.reshape(bsz, length, SSD_INNER) * ssd_norm_g.astype(jnp.float32)).astype(dtype)
    branch_s = ys @ w_sb
    merged = jax.nn.sigmoid(g_f) * branch_f + jax.nn.sigmoid(g_s) * branch_s
    return merged @ w_o


def _conv_ffn(h, w_up, conv_w, conv_b, w_down, on_grid):
    u = h @ w_up
    if on_grid:
        u = _dwconv_grid(u, conv_w, conv_b)
    else:
        u = _dwconv_seq(u, conv_w[FFN_CONV // 2], conv_b)
    gate, val = jnp.split(u, 2, axis=-1)
    return (jax.nn.silu(gate) * val) @ w_down


def setup_inputs(seed: int = 0) -> dict:
    key = jax.random.key(seed)
    ks = jax.random.split(key, 24)
    f32 = jnp.float32
    nrm = lambda k, shape, scale: jax.random.normal(k, shape, f32) * scale
    dt0 = jnp.exp(jax.random.uniform(ks[9], (DEPTH, 2, SSD_HEADS), f32, np.log(1e-3), np.log(1e-1)))
    return {
        "x": nrm(ks[0], (BATCH, SEQ, D_MODEL), 1.0),
        "c": nrm(ks[1], (BATCH, D_MODEL), 1.0),
        "ctx": nrm(ks[2], (BATCH, CTX_LEN, D_MODEL), 1.0),
        "c_ctx": nrm(ks[3], (D_MODEL,), 1.0),
        "w_mod": nrm(ks[4], (DEPTH, D_MODEL, N_MOD * D_MODEL), D_MODEL ** -0.5),
        "b_mod": nrm(ks[5], (DEPTH, N_MOD * D_MODEL), 0.01),
        "norm1_g": 1.0 + nrm(ks[6], (DEPTH, D_MODEL), 0.01),
        "w_in": nrm(ks[7], (DEPTH, D_MODEL, IN_TOTAL), D_MODEL ** -0.5),
        "conv_ssd_w": nrm(ks[8], (DEPTH, SSD_CONV, XBC_WIDTH), SSD_CONV ** -0.5),
        "conv_ssd_b": nrm(ks[10], (DEPTH, XBC_WIDTH), 0.01),
        "dt_bias": dt0 + jnp.log(-jnp.expm1(-dt0)),
        "a_log": jnp.log(jax.random.uniform(ks[11], (DEPTH, 2, SSD_HEADS), f32, 1.0, 16.0)),
        "d_skip": 1.0 + nrm(ks[12], (DEPTH, SSD_HEADS), 0.01),
        "ssd_norm_g": 1.0 + nrm(ks[13], (DEPTH, SSD_INNER), 0.01),
        "w_fa": nrm(ks[14], (DEPTH, F_WIDTH, D_MODEL), F_WIDTH ** -0.5),
        "w_sb": nrm(ks[15], (DEPTH, SSD_INNER, D_MODEL), SSD_INNER ** -0.5),
        "w_o": nrm(ks[16], (DEPTH, D_MODEL, D_MODEL), D_MODEL ** -0.5),
        "norm2_g": 1.0 + nrm(ks[17], (DEPTH, D_MODEL), 0.01),
        "w_up": nrm(ks[18], (DEPTH, D_MODEL, 2 * D_FF), D_MODEL ** -0.5),
        "conv_ffn_w": nrm(ks[19], (DEPTH, FFN_CONV, FFN_CONV, 2 * D_FF), 1.0 / FFN_CONV),
        "conv_ffn_b": nrm(ks[20], (DEPTH, 2 * D_FF), 0.01),
        "w_down": nrm(ks[21], (DEPTH, D_FF, D_MODEL), D_FF ** -0.5),
        "final_g": 1.0 + nrm(ks[22], (D_MODEL,), 0.01),
    }


def reference(x, c, ctx, c_ctx, w_mod, b_mod, norm1_g, w_in, conv_ssd_w, conv_ssd_b, dt_bias, a_log,
              d_skip, ssd_norm_g, w_fa, w_sb, w_o, norm2_g, w_up, conv_ffn_w, conv_ffn_b, w_down, final_g):
    bsz = x.shape[0]
    lat, cstream = x, ctx
    for l in range(DEPTH):
        last = l == DEPTH - 1
        m_lat = _adaln(c, w_mod[l], b_mod[l])
        m_ctx = _adaln(c_ctx, w_mod[l], b_mod[l])
        a = -jnp.exp(a_log[l].astype(jnp.float32)).reshape(2, SSD_GROUPS, SSD_HPG)
        h_lat = _modulate(_rmsnorm(lat, norm1_g[l]), m_lat[0], m_lat[1])
        h_ctx = _modulate(_rmsnorm(cstream, norm1_g[l]), m_ctx[0], m_ctx[1])
        f_l, xbc_l, z_l, dt_l, gf_l, gs_l = _split_in(h_lat @ w_in[l])
        f_c, xbc_c, z_c, dt_c, gf_c, gs_c = _split_in(h_ctx @ w_in[l])
        xh_c, b_c, c_c, dts_c = _ssd_prep(xbc_c, dt_c, conv_ssd_w[l], conv_ssd_b[l], dt_bias[l])
        zeros = jnp.zeros((bsz, SSD_GROUPS, SSD_HPG, SSD_HEAD_DIM, SSD_STATE), jnp.float32)
        y_c, s_f, s_b = _ssd_bidir(xh_c, b_c, c_c, dts_c, a, zeros, zeros)
        xh_l, b_l, c_l, dts_l = _ssd_prep(xbc_l, dt_l, conv_ssd_w[l], conv_ssd_b[l], dt_bias[l])
        y_l, _, _ = _ssd_bidir(xh_l, b_l, c_l, dts_l, a, s_f, s_b)
        mix_lat = _mixer_merge(f_l, y_l, xh_l, z_l, gf_l, gs_l, d_skip[l], ssd_norm_g[l],
                               w_fa[l], w_sb[l], w_o[l])
        lat = lat + m_lat[2] * mix_lat
        h2_lat = _modulate(_rmsnorm(lat, norm2_g[l]), m_lat[3], m_lat[4])
        lat = lat + m_lat[5] * _conv_ffn(h2_lat, w_up[l], conv_ffn_w[l], conv_ffn_b[l], w_down[l], True)
        if not last:
            mix_ctx = _mixer_merge(f_c, y_c, xh_c, z_c, gf_c, gs_c, d_skip[l], ssd_norm_g[l],
                                   w_fa[l], w_sb[l], w_o[l])
            cstream = cstream + m_ctx[2] * mix_ctx
            h2_ctx = _modulate(_rmsnorm(cstream, norm2_g[l]), m_ctx[3], m_ctx[4])
            cstream = cstream + m_ctx[5] * _conv_ffn(h2_ctx, w_up[l], conv_ffn_w[l], conv_ffn_b[l],
                                                     w_down[l], False)
    return _rmsnorm(lat, final_g)
```

```python
import functools

import numpy as np
import jax
import jax.numpy as jnp
from jax import lax
from jax.experimental import pallas as pl
from jax.experimental.pallas import tpu as pltpu

F32 = jnp.float32
BF16 = jnp.bfloat16

D_MODEL = 1024
N_MOD = 6
EPS = 1e-6
F_GROUPS = 8
F_GROUP_DIM = 128
SSD_HEADS = 32
SSD_HEAD_DIM = 64
SSD_INNER = SSD_HEADS * SSD_HEAD_DIM
SSD_GROUPS = 4
SSD_HPG = SSD_HEADS // SSD_GROUPS
SSD_STATE = 128
SSD_BC = SSD_GROUPS * SSD_STATE
SSD_CHUNK = 128
SSD_GW = SSD_HPG * SSD_HEAD_DIM
XBC_WIDTH = SSD_INNER + 2 * SSD_BC
D_FF = 2816
GRID_W = 64
LOG2E = 1.4426950408889634

LANES = 128
BF16_SUBLANES = 16
VMEM_LIMIT = 56 * 1024 * 1024

IN_TM = 512
IN_CHUNK = 512
CONV_TM = 512
CONV_CW = 512
FFN_TM = 1024
FFN_CW = 256
FNET_ROWS = 1024


def _cparams(sem):
    return pltpu.CompilerParams(dimension_semantics=sem, vmem_limit_bytes=VMEM_LIMIT)


def _sigmoid(v):
    return 1.0 / (1.0 + jnp.exp(-v))


def _dot(a, b):
    return jnp.dot(a, b, preferred_element_type=F32)


def _split2(v):
    hi = v.astype(BF16)
    lo = (v - hi.astype(F32)).astype(BF16)
    return hi, lo


def _split3(v):
    hi = v.astype(BF16)
    r = v - hi.astype(F32)
    mid = r.astype(BF16)
    lo = (r - mid.astype(F32)).astype(BF16)
    return hi, mid, lo


def _mod_kernel(c_ref, w_ref, b_ref, o_ref):
    c = c_ref[...]
    s = c * _sigmoid(c)
    w = w_ref[...]
    s_hi, s_lo = _split2(s)
    w_hi, w_lo = _split2(w)
    o_ref[...] = _dot(s_hi, w_hi) + _dot(s_lo, w_hi) + _dot(s_hi, w_lo) + b_ref[...]


def _adaln(cond8, w_mod, b_mod):
    n = w_mod.shape[1]
    tn = 1024
    return pl.pallas_call(
        _mod_kernel,
        out_shape=jax.ShapeDtypeStruct((8, n), F32),
        grid=(n // tn,),
        in_specs=[pl.BlockSpec((8, D_MODEL), lambda j: (0, 0)),
                  pl.BlockSpec((D_MODEL, tn), lambda j: (0, j)),
                  pl.BlockSpec((1, tn), lambda j: (0, j))],
        out_specs=pl.BlockSpec((8, tn), lambda j: (0, j)),
        compiler_params=_cparams(("parallel",)),
        name="adaln_mod",
    )(cond8, w_mod, b_mod)


def _rms_modulate(x, g, shift, scale):
    ms = jnp.mean(x * x, axis=-1, keepdims=True)
    h = (x * lax.rsqrt(ms + EPS)) * g
    return h * (1.0 + scale) + shift


def _inproj_kernel(segs, x_ref, sh_ref, sc_ref, g_ref, w_ref, dtb_ref, *o_refs):
    hb = _rms_modulate(x_ref[...], g_ref[...], sh_ref[0], sc_ref[0]).astype(BF16)
    col = 0
    for (kind, width), o_ref in zip(segs, o_refs):
        for c0 in range(0, width, IN_CHUNK):
            cw = min(IN_CHUNK, width - c0)
            r = _dot(hb, w_ref[:, col + c0:col + c0 + cw])
            if kind == "silu":
                r = r * _sigmoid(r)
            elif kind == "sigmoid":
                r = _sigmoid(r)
            elif kind == "softplus":
                r = r + dtb_ref[...]
                r = jnp.maximum(r, 0.0) + jnp.log(1.0 + jnp.exp(-jnp.abs(r)))
            o_ref[:, c0:c0 + cw] = r.astype(o_ref.dtype)
        col += width


def _inproj(x2, shift, scale, g, w, dtb, segs, out_dtypes, rows_per_batch):
    rows = x2.shape[0]
    tm = min(IN_TM, rows_per_batch)
    tiles_per_batch = rows_per_batch // tm
    nb = shift.shape[0]
    bidx = (lambda i: (i // tiles_per_batch, 0, 0)) if nb > 1 else (lambda i: (0, 0, 0))
    wtot = w.shape[1]
    return pl.pallas_call(
        functools.partial(_inproj_kernel, segs),
        out_shape=[jax.ShapeDtypeStruct((rows, wd), dt) for (_, wd), dt in zip(segs, out_dtypes)],
        grid=(rows // tm,),
        in_specs=[pl.BlockSpec((tm, D_MODEL), lambda i: (i, 0)),
                  pl.BlockSpec((1, 1, D_MODEL), bidx),
                  pl.BlockSpec((1, 1, D_MODEL), bidx),
                  pl.BlockSpec((1, D_MODEL), lambda i: (0, 0)),
                  pl.BlockSpec((D_MODEL, wtot), lambda i: (0, 0), pipeline_mode=pl.Buffered(1)),
                  pl.BlockSpec((1, LANES), lambda i: (0, 0))],
        out_specs=[pl.BlockSpec((tm, wd), lambda i: (i, 0)) for (_, wd) in segs],
        compiler_params=_cparams(("parallel",)),
        name="inproj",
    )(x2, shift, scale, g, w, dtb)


def _conv_kernel(tiles_per_seq, xm_ref, xp_ref, xn_ref, w_ref, b_ref, o_ref):
    i = pl.program_id(0)
    first = (i % tiles_per_seq) == 0
    last = (i % tiles_per_seq) == tiles_per_seq - 1
    tm = xm_ref.shape[0]
    row = lax.broadcasted_iota(jnp.int32, (tm, CONV_CW), 0)
    for c0 in range(0, XBC_WIDTH, CONV_CW):
        x = xm_ref[:, c0:c0 + CONV_CW].astype(F32)
        prev = xp_ref[:, c0:c0 + CONV_CW].astype(F32)[BF16_SUBLANES - 1:BF16_SUBLANES]
        nxt = xn_ref[:, c0:c0 + CONV_CW].astype(F32)[0:1]
        prev = jnp.where(first, 0.0, prev)
        nxt = jnp.where(last, 0.0, nxt)
        xl = jnp.where(row == 0, prev, pltpu.roll(x, 1, 0))
        xr = jnp.where(row == tm - 1, nxt, pltpu.roll(x, tm - 1, 0))
        w = w_ref[:, c0:c0 + CONV_CW]
        y = xl * w[0:1] + x * w[1:2] + xr * w[2:3] + b_ref[:, c0:c0 + CONV_CW]
        o_ref[:, c0:c0 + CONV_CW] = (y * _sigmoid(y)).astype(o_ref.dtype)


def _ssd_conv(xbc, w, b, rows_per_seq):
    rows = xbc.shape[0]
    tm = min(CONV_TM, rows_per_seq)
    tiles_per_seq = rows_per_seq // tm
    hb = tm // BF16_SUBLANES
    nhb = rows // BF16_SUBLANES
    return pl.pallas_call(
        functools.partial(_conv_kernel, tiles_per_seq),
        out_shape=jax.ShapeDtypeStruct((rows, XBC_WIDTH), BF16),
        grid=(rows // tm,),
        in_specs=[pl.BlockSpec((tm, XBC_WIDTH), lambda i: (i, 0)),
                  pl.BlockSpec((BF16_SUBLANES, XBC_WIDTH), lambda i: (jnp.maximum(i * hb - 1, 0), 0)),
                  pl.BlockSpec((BF16_SUBLANES, XBC_WIDTH), lambda i: (jnp.minimum((i + 1) * hb, nhb - 1), 0)),
                  pl.BlockSpec((3, XBC_WIDTH), lambda i: (0, 0)),
                  pl.BlockSpec((1, XBC_WIDTH), lambda i: (0, 0))],
        out_specs=pl.BlockSpec((tm, XBC_WIDTH), lambda i: (i, 0)),
        compiler_params=_cparams(("parallel",)),
        name="ssd_conv",
    )(xbc, xbc, xbc, w, b)


def _expand(v, e_ref, lo_col, hi_col):
    hi, lo = _split2(v)
    e = e_ref[:, lo_col:hi_col]
    return _dot(hi, e) + _dot(lo, e)


def _ssd_kernel(nc, x_ref, b_ref, c_ref, dt_ref, alog_ref, dsk_ref, e_ref, sf0_ref, sb0_ref,
                y_ref, sfo_ref, sbo_ref, sf_s, sb_s, store_s):
    g = pl.program_id(1)
    t = pl.program_id(2)
    q = SSD_CHUNK
    gw = SSD_GW
    row = lax.broadcasted_iota(jnp.int32, (q, LANES), 0)
    lane = lax.broadcasted_iota(jnp.int32, (q, LANES), 1)

    dtg = pltpu.roll(dt_ref[...], lax.rem(LANES - 2 * SSD_HPG * g, LANES), 1)
    dtg = jnp.where(lane < 2 * SSD_HPG, dtg, 0.0)
    a2 = -jnp.exp(alog_ref[0]) * LOG2E
    la = dtg * a2
    tri = jnp.where(row >= lane, 1.0, 0.0).astype(BF16)
    la_hi, la_mid, la_lo = _split3(la)
    cum = _dot(tri, la_hi) + _dot(tri, la_mid) + _dot(tri, la_lo)
    excl = cum - la
    tot = cum[q - 1:q, :]
    is_f = lane < SSD_HPG
    e_in = jnp.exp2(jnp.where(is_f, cum, tot - excl))
    w_out = jnp.exp2(jnp.where(is_f, tot - cum, excl)) * dtg

    bm = b_ref[...]
    xf = x_ref[...].astype(F32)

    def local_state(w_exp):
        xw = (xf * w_exp).astype(BF16)
        return lax.dot_general(bm, xw, (((0,), (0,)), ((), ())), preferred_element_type=F32)

    @pl.when(t == 0)
    def _():
        sb_s[...] = sb0_ref[0, 0]

    @pl.when(t < nc)
    def _():
        ci = nc - 1 - t
        store_s[ci] = sb_s[...].astype(BF16)
        w_b = _expand(w_out, e_ref, gw, 2 * gw)
        dec_b = _expand(e_in[0:8], e_ref, gw, 2 * gw)[0:1]
        sb_s[...] = dec_b * sb_s[...] + local_state(w_b)

    @pl.when(t == nc)
    def _():
        sf_s[...] = sf0_ref[0, 0]

    @pl.when(t >= nc)
    def _():
        ci = t - nc
        cm = c_ref[...]
        scores = lax.dot_general(cm, bm, (((1,), (1,)), ((), ())), preferred_element_type=F32)
        cum_t = cum.T
        excl_t = excl.T
        dt_t = dtg.T
        lower = row >= lane
        diag = row == lane
        ys = []
        for j in range(SSD_HPG // 2):
            ms = []
            for h in (2 * j, 2 * j + 1):
                hb = SSD_HPG + h
                expo = jnp.where(lower,
                                 cum[:, h:h + 1] - cum_t[h:h + 1, :],
                                 excl_t[hb:hb + 1, :] - excl[:, hb:hb + 1])
                dtf = dt_t[h:h + 1, :]
                dtb = dt_t[hb:hb + 1, :]
                dsel = jnp.where(diag, dtf + dtb, jnp.where(lower, dtf, dtb))
                ms.append((scores * jnp.exp2(expo) * dsel).astype(BF16))
            lhs = jnp.concatenate(ms, axis=1)
            xp = xf[:, j * LANES:(j + 1) * LANES]
            top = jnp.where(lane < SSD_HEAD_DIM, xp, 0.0).astype(BF16)
            bot = jnp.where(lane >= SSD_HEAD_DIM, xp, 0.0).astype(BF16)
            ys.append(_dot(lhs, jnp.concatenate([top, bot], axis=0)))
        y = jnp.concatenate(ys, axis=1)
        e_exp = _expand(e_in, e_ref, 0, 2 * gw)
        y = y + e_exp[:, :gw] * _dot(cm, sf_s[...].astype(BF16))
        y = y + e_exp[:, gw:] * _dot(cm, store_s[ci])
        y = y + dsk_ref[...] * xf
        y_ref[...] = y.astype(y_ref.dtype)
        w_f = _expand(w_out, e_ref, 0, gw)
        sf_s[...] = e_exp[q - 1:q, :gw] * sf_s[...] + local_state(w_f)

    @pl.when(t == 2 * nc - 1)
    def _():
        sfo_ref[0, 0] = sf_s[...]
        sbo_ref[0, 0] = sb_s[...]


def _ssd_scan(xa, dt, alog_g, dsk, expm, sf0, sb0, nb, length):
    nc = length // SSD_CHUNK
    rows = xa.shape[0]
    xoff = SSD_INNER // LANES

    def chunk(t):
        return jnp.where(t < nc, nc - 1 - t, t - nc)

    def ychunk(t):
        return jnp.maximum(t - nc, 0)

    state_shape = jax.ShapeDtypeStruct((nb, SSD_GROUPS, SSD_STATE, SSD_GW), F32)
    state_spec = pl.BlockSpec((1, 1, SSD_STATE, SSD_GW), lambda b, g, t: (b, g, 0, 0))
    return pl.pallas_call(
        functools.partial(_ssd_kernel, nc),
        out_shape=[jax.ShapeDtypeStruct((rows, SSD_INNER), BF16), state_shape, state_shape],
        grid=(nb, SSD_GROUPS, 2 * nc),
        in_specs=[pl.BlockSpec((SSD_CHUNK, SSD_GW), lambda b, g, t: (b * nc + chunk(t), g)),
                  pl.BlockSpec((SSD_CHUNK, SSD_STATE), lambda b, g, t: (b * nc + chunk(t), xoff + g)),
                  pl.BlockSpec((SSD_CHUNK, SSD_STATE),
                               lambda b, g, t: (b * nc + chunk(t), xoff + SSD_GROUPS + g)),
                  pl.BlockSpec((SSD_CHUNK, LANES), lambda b, g, t: (b * nc + chunk(t), 0)),
                  pl.BlockSpec((1, 1, LANES), lambda b, g, t: (g, 0, 0)),
                  pl.BlockSpec((1, SSD_GW), lambda b, g, t: (0, g)),
                  pl.BlockSpec((LANES, 2 * SSD_GW), lambda b, g, t: (0, 0)),
                  state_spec, state_spec],
        out_specs=[pl.BlockSpec((SSD_CHUNK, SSD_GW), lambda b, g, t: (b * nc + ychunk(t), g)),
                   state_spec, state_spec],
        scratch_shapes=[pltpu.VMEM((SSD_STATE, SSD_GW), F32),
                        pltpu.VMEM((SSD_STATE, SSD_GW), F32),
                        pltpu.VMEM((nc, SSD_STATE, SSD_GW), BF16)],
        compiler_params=_cparams(("parallel", "parallel", "arbitrary")),
        name="ssd_scan",
    )(xa, xa, xa, dt, alog_g, dsk, expm, sf0, sb0)


def _fnet_kernel(n1, u_ref, ctab_ref, g_ref, f2_ref, o_ref, zr_s, zi_s, ar_s, ai_s):
    n2 = F_GROUP_DIM
    length = n1 * n2
    rows = min(FNET_ROWS, length)

    def stage0(i, carry):
        r0 = pl.multiple_of(i * rows, rows)
        pq = _dot(u_ref[pl.ds(r0, rows), :], ctab_ref[...])
        zr_s[pl.ds(r0, rows), :] = pq[:, :n2]
        zi_s[pl.ds(r0, rows), :] = pq[:, n2:]
        return carry

    lax.fori_loop(0, length // rows, stage0, 0)

    def stage1(b, carry):
        z = jnp.concatenate([zr_s[pl.ds(b, n1, stride=n2), :],
                             zi_s[pl.ds(b, n1, stride=n2), :]], axis=0).astype(BF16)
        a = _dot(g_ref[b], z)
        ar_s[pl.ds(b, n1, stride=n2), :] = a[:n1]
        ai_s[pl.ds(b, n1, stride=n2), :] = a[n1:]
        return carry

    lax.fori_loop(0, n2, stage1, 0)

    def stage2(k1, carry):
        r0 = pl.multiple_of(k1 * n2, n2)
        a = jnp.concatenate([ar_s[pl.ds(r0, n2), :], ai_s[pl.ds(r0, n2), :]], axis=0).astype(BF16)
        o_ref[pl.ds(k1, n2, stride=n1), :] = _dot(f2_ref[...], a)
        return carry

    lax.fori_loop(0, n1, stage2, 0)


def _fnet_tables(length):
    n2 = F_GROUP_DIM
    n1 = length // n2
    j = np.arange(n2, dtype=np.int64)
    ang_c = 2.0 * np.pi * ((j[:, None] * j[None, :]) % n2) / n2
    ctab = np.concatenate([np.cos(ang_c), -np.sin(ang_c)], axis=1)
    k1 = np.arange(n1, dtype=np.int64)[None, :, None]
    a = np.arange(n1, dtype=np.int64)[None, None, :]
    b = np.arange(n2, dtype=np.int64)[:, None, None]
    ang = 2.0 * np.pi * ((k1 * (n2 * a + b)) % length) / length
    cs, sn = np.cos(ang), np.sin(ang)
    gtab = np.concatenate([np.concatenate([cs, sn], axis=2),
                           np.concatenate([-sn, cs], axis=2)], axis=1)
    scale = 1.0 / np.sqrt(float(length * n2))
    f2 = np.concatenate([np.cos(ang_c), np.sin(ang_c)], axis=1) * scale
    as_bf16 = lambda t: jnp.asarray(t.astype(np.float32)).astype(BF16)
    return as_bf16(ctab), as_bf16(gtab), as_bf16(f2)


def _fnet(u, nb, length):
    n2 = F_GROUP_DIM
    n1 = length // n2
    ctab, gtab, f2 = _fnet_tables(length)
    scr = pltpu.VMEM((length, n2), F32)
    return pl.pallas_call(
        functools.partial(_fnet_kernel, n1),
        out_shape=jax.ShapeDtypeStruct((nb * length, F_GROUPS * n2), F32),
        grid=(nb, F_GROUPS),
        in_specs=[pl.BlockSpec((length, n2), lambda b, g: (b, g)),
                  pl.BlockSpec((n2, 2 * n2), lambda b, g: (0, 0)),
                  pl.BlockSpec((n2, 2 * n1, 2 * n1), lambda b, g: (0, 0, 0),
                               pipeline_mode=pl.Buffered(1)),
                  pl.BlockSpec((n2, 2 * n2), lambda b, g: (0, 0))],
        out_specs=pl.BlockSpec((length, n2), lambda b, g: (b, g)),
        scratch_shapes=[scr, scr, scr, scr],
        compiler_params=_cparams(("parallel", "parallel")),
        name="fnet",
    )(u, ctab, gtab, f2)


def _merge_kernel(x_ref, mix_ref, y_ref, zs_ref, sgf_ref, sgs_ref, wfa_ref, wsb_ref, wo_ref,
                  ng_ref, gate_ref, n2g_ref, sh_ref, sc_ref, lat_ref, h2_ref):
    branch_f = _dot(mix_ref[...].astype(BF16), wfa_ref[...])
    parts = []
    for g in range(SSD_GROUPS):
        sl = slice(g * SSD_GW, (g + 1) * SSD_GW)
        v = y_ref[:, sl].astype(F32) * zs_ref[:, sl].astype(F32)
        v = v * lax.rsqrt(jnp.mean(v * v, axis=-1, keepdims=True) + EPS)
        parts.append((v * ng_ref[:, sl]).astype(BF16))
    branch_s = _dot(jnp.concatenate(parts, axis=1), wsb_ref[...])
    merged = sgf_ref[...].astype(F32) * branch_f + sgs_ref[...].astype(F32) * branch_s
    lat = x_ref[...] + gate_ref[0] * _dot(merged.astype(BF16), wo_ref[...])
    lat_ref[...] = lat
    h2_ref[...] = _rms_modulate(lat, n2g_ref[...], sh_ref[0], sc_ref[0]).astype(h2_ref.dtype)


def _merge(x2, mixed, y, zs, sgf, sgs, wfa, wsb, wo, ng, gate, n2g, shift, scale, rows_per_batch):
    rows = x2.shape[0]
    tm = min(IN_TM, rows_per_batch)
    tpb = rows_per_batch // tm
    row_spec = lambda w: pl.BlockSpec((tm, w), lambda i: (i, 0))
    full = lambda a: pl.BlockSpec(a.shape, lambda i: (0,) * a.ndim, pipeline_mode=pl.Buffered(1))
    bspec = pl.BlockSpec((1, 1, D_MODEL), lambda i: (i // tpb, 0, 0))
    return pl.pallas_call(
        _merge_kernel,
        out_shape=[jax.ShapeDtypeStruct((rows, D_MODEL), F32),
                   jax.ShapeDtypeStruct((rows, D_MODEL), BF16)],
        grid=(rows // tm,),
        in_specs=[row_spec(D_MODEL), row_spec(D_MODEL), row_spec(SSD_INNER), row_spec(SSD_INNER),
                  row_spec(D_MODEL), row_spec(D_MODEL), full(wfa), full(wsb), full(wo),
                  pl.BlockSpec((1, SSD_INNER), lambda i: (0, 0)), bspec,
                  pl.BlockSpec((1, D_MODEL), lambda i: (0, 0)), bspec, bspec],
        out_specs=[row_spec(D_MODEL), row_spec(D_MODEL)],
        compiler_params=_cparams(("parallel",)),
        name="merge",
    )(x2, mixed, y, zs, sgf, sgs, wfa, wsb, wo, ng, gate, n2g, shift, scale)


def _ffn_kernel(tiles_per_img, hm_ref, hp_ref, hn_ref, wg_ref, wv_ref, cwg_ref, cwv_ref,
                cbg_ref, cbv_ref, wd_ref, lat_ref, gate_ref, fg_ref, o_ref, hext_s, acc_s):
    i = pl.program_id(0)
    j = pl.program_id(1)
    tm = hm_ref.shape[0]
    ext = tm + 2 * GRID_W

    @pl.when(j == 0)
    def _():
        first = (i % tiles_per_img) == 0
        last = (i % tiles_per_img) == tiles_per_img - 1
        hp = hp_ref[...]
        hn = hn_ref[...]
        hext_s[0:GRID_W, :] = jnp.where(first, jnp.zeros_like(hp), hp)
        hext_s[GRID_W:GRID_W + tm, :] = hm_ref[...]
        hext_s[GRID_W + tm:ext, :] = jnp.where(last, jnp.zeros_like(hn), hn)
        acc_s[...] = jnp.zeros_like(acc_s)

    he = hext_s[...]
    colpos = lax.rem(lax.broadcasted_iota(jnp.int32, (ext, FFN_CW), 0), GRID_W)

    def branch(w_ref, cw_ref, cb_ref):
        u = _dot(he, w_ref[0])
        ul = jnp.where(colpos == 0, 0.0, pltpu.roll(u, 1, 0))
        ur = jnp.where(colpos == GRID_W - 1, 0.0, pltpu.roll(u, ext - 1, 0))
        cw = cw_ref[0]
        out = cb_ref[0]
        for ky in range(3):
            s = slice(ky * GRID_W, ky * GRID_W + tm)
            out = out + ul[s] * cw[3 * ky:3 * ky + 1] + u[s] * cw[3 * ky + 1:3 * ky + 2] \
                + ur[s] * cw[3 * ky + 2:3 * ky + 3]
        return out

    gate = branch(wg_ref, cwg_ref, cbg_ref)
    val = branch(wv_ref, cwv_ref, cbv_ref)
    act = (gate * _sigmoid(gate) * val).astype(BF16)
    acc_s[...] += _dot(act, wd_ref[0])

    @pl.when(j == pl.num_programs(1) - 1)
    def _():
        lat = lat_ref[...] + gate_ref[0] * acc_s[...]
        ms = jnp.mean(lat * lat, axis=-1, keepdims=True)
        o_ref[...] = (lat * lax.rsqrt(ms + EPS)) * fg_ref[...]


def _conv_ffn(h2, wup3, cw3, cb3, wd3, lat1, gate2, fg, rows_per_img):
    rows = h2.shape[0]
    tm = min(FFN_TM, rows_per_img)
    tiles_per_img = rows_per_img // tm
    nj = D_FF // FFN_CW
    hb = tm // GRID_W
    nhb = rows // GRID_W
    return pl.pallas_call(
        functools.partial(_ffn_kernel, tiles_per_img),
        out_shape=jax.ShapeDtypeStruct((rows, D_MODEL), F32),
        grid=(rows // tm, nj),
        in_specs=[pl.BlockSpec((tm, D_MODEL), lambda i, j: (i, 0)),
                  pl.BlockSpec((GRID_W, D_MODEL), lambda i, j: (jnp.maximum(i * hb - 1, 0), 0)),
                  pl.BlockSpec((GRID_W, D_MODEL), lambda i, j: (jnp.minimum((i + 1) * hb, nhb - 1), 0)),
                  pl.BlockSpec((1, D_MODEL, FFN_CW), lambda i, j: (j, 0, 0)),
                  pl.BlockSpec((1, D_MODEL, FFN_CW), lambda i, j: (nj + j, 0, 0)),
                  pl.BlockSpec((1, 9, FFN_CW), lambda i, j: (j, 0, 0)),
                  pl.BlockSpec((1, 9, FFN_CW), lambda i, j: (nj + j, 0, 0)),
                  pl.BlockSpec((1, 1, FFN_CW), lambda i, j: (j, 0, 0)),
                  pl.BlockSpec((1, 1, FFN_CW), lambda i, j: (nj + j, 0, 0)),
                  pl.BlockSpec((1, FFN_CW, D_MODEL), lambda i, j: (j, 0, 0)),
                  pl.BlockSpec((tm, D_MODEL), lambda i, j: (i, 0)),
                  pl.BlockSpec((1, 1, D_MODEL), lambda i, j: (i // tiles_per_img, 0, 0)),
                  pl.BlockSpec((1, D_MODEL), lambda i, j: (0, 0))],
        out_specs=pl.BlockSpec((tm, D_MODEL), lambda i, j: (i, 0)),
        scratch_shapes=[pltpu.VMEM((tm + 2 * GRID_W, D_MODEL), BF16),
                        pltpu.VMEM((tm, D_MODEL), F32)],
        compiler_params=_cparams(("parallel", "arbitrary")),
        name="conv_ffn",
    )(h2, h2, h2, wup3, wup3, cw3, cw3, cb3, cb3, wd3, lat1, gate2, fg)


def _dt_perm():
    idx = np.arange(2 * SSD_HEADS).reshape(2, SSD_GROUPS, SSD_HPG)
    return idx.transpose(1, 0, 2).reshape(-1)


def _expansion_matrix():
    e = np.zeros((LANES, 2 * SSD_GW), np.float32)
    for d in range(2):
        for h in range(SSD_HPG):
            c0 = d * SSD_GW + h * SSD_HEAD_DIM
            e[d * SSD_HPG + h, c0:c0 + SSD_HEAD_DIM] = 1.0
    return jnp.asarray(e).astype(BF16)


def kernel(x, c, ctx, c_ctx, w_mod, b_mod, norm1_g, w_in, conv_ssd_w, conv_ssd_b, dt_bias, a_log,
           d_skip, ssd_norm_g, w_fa, w_sb, w_o, norm2_g, w_up, conv_ffn_w, conv_ffn_b, w_down, final_g):
    assert w_mod.shape[0] == 1, "single-layer block"
    nb, length, _ = x.shape
    clen = ctx.shape[1]
    assert length % FFN_TM == 0 and clen % SSD_CHUNK == 0

    cond = jnp.zeros((8, D_MODEL), F32).at[:nb].set(c).at[nb].set(c_ctx)
    mods = _adaln(cond, w_mod[0], b_mod[0][None, :])
    m_lat = [mods[:nb, k * D_MODEL:(k + 1) * D_MODEL][:, None, :] for k in range(N_MOD)]
    m_ctx = [mods[nb:nb + 1, k * D_MODEL:(k + 1) * D_MODEL][:, None, :] for k in range(2)]

    w = w_in[0]
    o_f, o_xbc, o_z = D_MODEL, D_MODEL + XBC_WIDTH, D_MODEL + XBC_WIDTH + SSD_INNER
    o_dt = o_z + 2 * SSD_HEADS
    perm = _dt_perm()
    w_dt = jnp.pad(w[:, o_z:o_dt][:, perm], ((0, 0), (0, LANES - 2 * SSD_HEADS)))
    w_lat = jnp.concatenate([w[:, :o_f], w[:, o_f:o_xbc], w[:, o_xbc:o_z], w[:, o_dt:o_dt + D_MODEL],
                             w[:, o_dt + D_MODEL:], w_dt], axis=1).astype(BF16)
    w_ctx = jnp.concatenate([w[:, o_f:o_xbc], w_dt], axis=1).astype(BF16)
    dtb = jnp.pad(dt_bias[0].reshape(-1)[perm], (0, LANES - 2 * SSD_HEADS))[None, :]
    g1 = norm1_g[0][None, :]

    x2 = x.reshape(nb * length, D_MODEL)
    segs_lat = (("id", D_MODEL), ("id", XBC_WIDTH), ("silu", SSD_INNER), ("sigmoid", D_MODEL),
                ("sigmoid", D_MODEL), ("softplus", LANES))
    u_f, xbc_l, zs, sgf, sgs, dt_l = _inproj(
        x2, m_lat[0], m_lat[1], g1, w_lat, dtb, segs_lat, (BF16, BF16, BF16, BF16, BF16, F32), length)
    segs_ctx = (("id", XBC_WIDTH), ("softplus", LANES))
    xbc_c, dt_c = _inproj(ctx.reshape(nb * clen, D_MODEL), m_ctx[0], m_ctx[1], g1, w_ctx, dtb,
                          segs_ctx, (BF16, F32), clen)

    cw, cb = conv_ssd_w[0], conv_ssd_b[0][None, :]
    alog_g = jnp.pad(a_log[0].reshape(2, SSD_GROUPS, SSD_HPG).transpose(1, 0, 2).reshape(SSD_GROUPS, -1),
                     ((0, 0), (0, LANES - 2 * SSD_HPG)))[:, None, :]
    dsk = jnp.repeat(d_skip[0], SSD_HEAD_DIM)[None, :]
    expm = _expansion_matrix()
    zeros = jnp.zeros((nb, SSD_GROUPS, SSD_STATE, SSD_GW), F32)
    _, s_f, s_b = _ssd_scan(_ssd_conv(xbc_c, cw, cb, clen), dt_c, alog_g, dsk, expm, zeros, zeros, nb, clen)
    y, _, _ = _ssd_scan(_ssd_conv(xbc_l, cw, cb, length), dt_l, alog_g, dsk, expm, s_f, s_b, nb, length)

    mixed = _fnet(u_f, nb, length)

    lat1, h2 = _merge(x2, mixed, y, zs, sgf, sgs, w_fa[0].astype(BF16), w_sb[0].astype(BF16),
                      w_o[0].astype(BF16), ssd_norm_g[0][None, :], m_lat[2], norm2_g[0][None, :],
                      m_lat[3], m_lat[4], length)

    nj = D_FF // FFN_CW
    wup3 = w_up[0].astype(BF16).reshape(D_MODEL, 2 * nj, FFN_CW).transpose(1, 0, 2)
    cw3 = conv_ffn_w[0].reshape(9, 2 * nj, FFN_CW).transpose(1, 0, 2)
    cb3 = conv_ffn_b[0].reshape(2 * nj, 1, FFN_CW)
    wd3 = w_down[0].astype(BF16).reshape(nj, FFN_CW, D_MODEL)
    out = _conv_ffn(h2, wup3, cw3, cb3, wd3, lat1, m_lat[5], final_g[None, :], length)
    return out.reshape(nb, length, D_MODEL)
```

```python
import functools

import numpy as np
import jax
import jax.numpy as jnp
from jax import lax
from jax.experimental import pallas as pl
from jax.experimental.pallas import tpu as pltpu

F32 = jnp.float32
BF16 = jnp.bfloat16

D_MODEL = 1024
N_MOD = 6
EPS = 1e-6
F_GROUPS = 8
F_GROUP_DIM = 128
SSD_HEADS = 32
SSD_HEAD_DIM = 64
SSD_INNER = SSD_HEADS * SSD_HEAD_DIM
SSD_GROUPS = 4
SSD_HPG = SSD_HEADS // SSD_GROUPS
SSD_STATE = 128
SSD_BC = SSD_GROUPS * SSD_STATE
SSD_CHUNK = 128
SSD_GW = SSD_HPG * SSD_HEAD_DIM
XBC_WIDTH = SSD_INNER + 2 * SSD_BC
D_FF = 2816
GRID_W = 64
LOG2E = 1.4426950408889634

LANES = 128
F32_SUBLANES = 8
BF16_SUBLANES = 16
VMEM_LIMIT = 56 * 1024 * 1024

IN_TM = 512
IN_CHUNK = 512
CONV_TM = 512
CONV_CW = 512
FFN_TM = 1024
FFN_CW = 256
FNET_SLABS = 8
FNET_PAD = 8
FNET_UNROLL = 8


def _cparams(sem):
    return pltpu.CompilerParams(dimension_semantics=sem, vmem_limit_bytes=VMEM_LIMIT)


def _sigmoid(v):
    return 1.0 / (1.0 + jnp.exp(-v))


def _dot(a, b):
    return jnp.dot(a, b, preferred_element_type=F32)


def _split2(v):
    hi = v.astype(BF16)
    lo = (v - hi.astype(F32)).astype(BF16)
    return hi, lo


def _split3(v):
    hi = v.astype(BF16)
    r = v - hi.astype(F32)
    mid = r.astype(BF16)
    lo = (r - mid.astype(F32)).astype(BF16)
    return hi, mid, lo


def _mod_kernel(c_ref, w_ref, b_ref, o_ref):
    c = c_ref[...]
    s = c * _sigmoid(c)
    w = w_ref[...]
    s_hi, s_lo = _split2(s)
    w_hi, w_lo = _split2(w)
    o_ref[...] = _dot(s_hi, w_hi) + _dot(s_lo, w_hi) + _dot(s_hi, w_lo) + b_ref[...]


def _adaln(cond8, w_mod, b_mod):
    n = w_mod.shape[1]
    tn = 1024
    return pl.pallas_call(
        _mod_kernel,
        out_shape=jax.ShapeDtypeStruct((8, n), F32),
        grid=(n // tn,),
        in_specs=[pl.BlockSpec((8, D_MODEL), lambda j: (0, 0)),
                  pl.BlockSpec((D_MODEL, tn), lambda j: (0, j)),
                  pl.BlockSpec((1, tn), lambda j: (0, j))],
        out_specs=pl.BlockSpec((8, tn), lambda j: (0, j)),
        compiler_params=_cparams(("parallel",)),
        name="adaln_mod",
    )(cond8, w_mod, b_mod)


def _rms_modulate(x, g, shift, scale):
    ms = jnp.mean(x * x, axis=-1, keepdims=True)
    h = (x * lax.rsqrt(ms + EPS)) * g
    return h * (1.0 + scale) + shift


def _inproj_kernel(segs, x_ref, sh_ref, sc_ref, g_ref, w_ref, dtb_ref, *o_refs):
    hb = _rms_modulate(x_ref[...], g_ref[...], sh_ref[0], sc_ref[0]).astype(BF16)
    col = 0
    for (kind, width), o_ref in zip(segs, o_refs):
        for c0 in range(0, width, IN_CHUNK):
            cw = min(IN_CHUNK, width - c0)
            r = _dot(hb, w_ref[:, col + c0:col + c0 + cw])
            if kind == "silu":
                r = r * _sigmoid(r)
            elif kind == "sigmoid":
                r = _sigmoid(r)
            elif kind == "softplus":
                r = r + dtb_ref[...]
                r = jnp.maximum(r, 0.0) + jnp.log(1.0 + jnp.exp(-jnp.abs(r)))
            o_ref[:, c0:c0 + cw] = r.astype(o_ref.dtype)
        col += width


def _inproj(x2, shift, scale, g, w, dtb, segs, out_dtypes, rows_per_batch):
    rows = x2.shape[0]
    tm = min(IN_TM, rows_per_batch)
    tiles_per_batch = rows_per_batch // tm
    nb = shift.shape[0]
    bidx = (lambda i: (i // tiles_per_batch, 0, 0)) if nb > 1 else (lambda i: (0, 0, 0))
    wtot = w.shape[1]
    return pl.pallas_call(
        functools.partial(_inproj_kernel, segs),
        out_shape=[jax.ShapeDtypeStruct((rows, wd), dt) for (_, wd), dt in zip(segs, out_dtypes)],
        grid=(rows // tm,),
        in_specs=[pl.BlockSpec((tm, D_MODEL), lambda i: (i, 0)),
                  pl.BlockSpec((1, 1, D_MODEL), bidx),
                  pl.BlockSpec((1, 1, D_MODEL), bidx),
                  pl.BlockSpec((1, D_MODEL), lambda i: (0, 0)),
                  pl.BlockSpec((D_MODEL, wtot), lambda i: (0, 0), pipeline_mode=pl.Buffered(1)),
                  pl.BlockSpec((1, LANES), lambda i: (0, 0))],
        out_specs=[pl.BlockSpec((tm, wd), lambda i: (i, 0)) for (_, wd) in segs],
        compiler_params=_cparams(("parallel",)),
        name="inproj",
    )(x2, shift, scale, g, w, dtb)


def _conv_kernel(tiles_per_seq, xm_ref, xp_ref, xn_ref, w_ref, b_ref, o_ref):
    i = pl.program_id(0)
    first = (i % tiles_per_seq) == 0
    last = (i % tiles_per_seq) == tiles_per_seq - 1
    tm = xm_ref.shape[0]
    nt = tm // F32_SUBLANES
    sub = lax.broadcasted_iota(jnp.int32, (nt, F32_SUBLANES, CONV_CW), 1)
    for c0 in range(0, XBC_WIDTH, CONV_CW):
        x = xm_ref[:, c0:c0 + CONV_CW].astype(F32).reshape(nt, F32_SUBLANES, CONV_CW)
        prev = xp_ref[:, c0:c0 + CONV_CW].astype(F32)[BF16_SUBLANES - 1:BF16_SUBLANES]
        nxt = xn_ref[:, c0:c0 + CONV_CW].astype(F32)[0:1]
        prev = jnp.where(first, 0.0, prev)
        nxt = jnp.where(last, 0.0, nxt)
        prev = jnp.broadcast_to(prev[None], (1, F32_SUBLANES, CONV_CW))
        nxt = jnp.broadcast_to(nxt[None], (1, F32_SUBLANES, CONV_CW))
        rl = pltpu.roll(x, 1, 1)
        xl = jnp.where(sub == 0, jnp.concatenate([prev, rl[:-1]], axis=0), rl)
        rr = pltpu.roll(x, F32_SUBLANES - 1, 1)
        xr = jnp.where(sub == F32_SUBLANES - 1, jnp.concatenate([rr[1:], nxt], axis=0), rr)
        w = w_ref[:, c0:c0 + CONV_CW]
        y = xl * w[0:1] + x * w[1:2] + xr * w[2:3] + b_ref[:, c0:c0 + CONV_CW]
        y = (y * _sigmoid(y)).reshape(tm, CONV_CW)
        o_ref[:, c0:c0 + CONV_CW] = y.astype(o_ref.dtype)


def _ssd_conv(xbc, w, b, rows_per_seq):
    rows = xbc.shape[0]
    tm = min(CONV_TM, rows_per_seq)
    tiles_per_seq = rows_per_seq // tm
    hb = tm // BF16_SUBLANES
    nhb = rows // BF16_SUBLANES
    return pl.pallas_call(
        functools.partial(_conv_kernel, tiles_per_seq),
        out_shape=jax.ShapeDtypeStruct((rows, XBC_WIDTH), BF16),
        grid=(rows // tm,),
        in_specs=[pl.BlockSpec((tm, XBC_WIDTH), lambda i: (i, 0)),
                  pl.BlockSpec((BF16_SUBLANES, XBC_WIDTH), lambda i: (jnp.maximum(i * hb - 1, 0), 0)),
                  pl.BlockSpec((BF16_SUBLANES, XBC_WIDTH), lambda i: (jnp.minimum((i + 1) * hb, nhb - 1), 0)),
                  pl.BlockSpec((3, XBC_WIDTH), lambda i: (0, 0)),
                  pl.BlockSpec((1, XBC_WIDTH), lambda i: (0, 0))],
        out_specs=pl.BlockSpec((tm, XBC_WIDTH), lambda i: (i, 0)),
        compiler_params=_cparams(("parallel",)),
        name="ssd_conv",
    )(xbc, xbc, xbc, w, b)


def _expand(v, e_ref):
    hi, lo = _split2(v)
    e = e_ref[...]
    return _dot(hi, e) + _dot(lo, e)


def _chunk_decays(dt_ref, alog_ref):
    q = SSD_CHUNK
    row = lax.broadcasted_iota(jnp.int32, (q, LANES), 0)
    lane = lax.broadcasted_iota(jnp.int32, (q, LANES), 1)
    dtv = jnp.where(lane < 2 * SSD_HEADS, dt_ref[...], 0.0)
    la = dtv * (-jnp.exp(alog_ref[...]) * LOG2E)
    tri = jnp.where(row >= lane, 1.0, 0.0).astype(BF16)
    la_hi, la_mid, la_lo = _split3(la)
    cum = _dot(tri, la_hi) + _dot(tri, la_mid) + _dot(tri, la_lo)
    excl = cum - la
    tot = cum[q - 1:q, :]
    is_f = (lane & SSD_HPG) == 0
    e_in = jnp.exp2(jnp.where(is_f, cum, tot - excl))
    w_out = jnp.exp2(jnp.where(is_f, tot - cum, excl)) * dtv
    return dtv, cum, excl, e_in, w_out


def _local_state(bm, xw):
    return lax.dot_general(bm, xw, (((0,), (0,)), ((), ())), preferred_element_type=F32)


def _ssd_bwd_kernel(nc, x_ref, b_ref, dt_ref, alog_ref, eb_ref, sb0_ref, enter_ref, sbo_ref, sb_s):
    t = pl.program_id(1)

    @pl.when(t == 0)
    def _():
        sb_s[...] = sb0_ref[0]

    enter_ref[0] = sb_s[...].astype(BF16)
    _, _, _, e_in, w_out = _chunk_decays(dt_ref, alog_ref)
    w_b = _expand(w_out, eb_ref)
    dec = _expand(e_in[0:8], eb_ref)[0:1]
    xw = (x_ref[...].astype(F32) * w_b).astype(BF16)
    for g in range(SSD_GROUPS):
        sl = slice(g * SSD_GW, (g + 1) * SSD_GW)
        loc = _local_state(b_ref[:, g * SSD_STATE:(g + 1) * SSD_STATE], xw[:, sl])
        sb_s[g] = dec[:, sl] * sb_s[g] + loc

    @pl.when(t == nc - 1)
    def _():
        sbo_ref[0] = sb_s[...]


def _ssd_fwd_kernel(nc, x_ref, b_ref, c_ref, dt_ref, alog_ref, dsk_ref, ef_ref, eb_ref, sf0_ref,
                    enter_ref, y_ref, sfo_ref, sf_s):
    t = pl.program_id(1)
    q = SSD_CHUNK

    @pl.when(t == 0)
    def _():
        sf_s[...] = sf0_ref[0]

    row = lax.broadcasted_iota(jnp.int32, (q, LANES), 0)
    lane = lax.broadcasted_iota(jnp.int32, (q, LANES), 1)
    lower = row >= lane
    diag = row == lane
    dtv, cum, excl, e_in, w_out = _chunk_decays(dt_ref, alog_ref)
    cum_t = cum.T
    excl_t = excl.T
    dt_t = dtv.T
    ef_exp = _expand(e_in, ef_ref)
    eb_exp = _expand(e_in, eb_ref)
    wf_exp = _expand(w_out, ef_ref)
    for g in range(SSD_GROUPS):
        sl = slice(g * SSD_GW, (g + 1) * SSD_GW)
        bm = b_ref[:, g * SSD_STATE:(g + 1) * SSD_STATE]
        cm = c_ref[:, g * SSD_STATE:(g + 1) * SSD_STATE]
        xf = x_ref[:, sl].astype(F32)
        scores = lax.dot_general(cm, bm, (((1,), (1,)), ((), ())), preferred_element_type=F32)
        ys = []
        for j in range(SSD_HPG // 2):
            ms = []
            for h in (2 * j, 2 * j + 1):
                lf = 2 * SSD_HPG * g + h
                lb = lf + SSD_HPG
                expo = jnp.where(lower,
                                 cum[:, lf:lf + 1] - cum_t[lf:lf + 1, :],
                                 excl_t[lb:lb + 1, :] - excl[:, lb:lb + 1])
                dtf = dt_t[lf:lf + 1, :]
                dtb = dt_t[lb:lb + 1, :]
                dsel = jnp.where(diag, dtf + dtb, jnp.where(lower, dtf, dtb))
                ms.append((scores * jnp.exp2(expo) * dsel).astype(BF16))
            xp = xf[:, j * LANES:(j + 1) * LANES]
            top = jnp.where(lane < SSD_HEAD_DIM, xp, 0.0).astype(BF16)
            bot = jnp.where(lane >= SSD_HEAD_DIM, xp, 0.0).astype(BF16)
            ys.append(_dot(jnp.concatenate(ms, axis=1), jnp.concatenate([top, bot], axis=0)))
        y = jnp.concatenate(ys, axis=1)
        y = y + ef_exp[:, sl] * _dot(cm, sf_s[g].astype(BF16))
        y = y + eb_exp[:, sl] * _dot(cm, enter_ref[0, g])
        y = y + dsk_ref[:, sl] * xf
        y_ref[:, sl] = y.astype(y_ref.dtype)
        xw = (xf * wf_exp[:, sl]).astype(BF16)
        sf_s[g] = ef_exp[q - 1:q, sl] * sf_s[g] + _local_state(bm, xw)

    @pl.when(t == nc - 1)
    def _():
        sfo_ref[0] = sf_s[...]


def _ssd_scan(xa, dt, alog, dsk, exp_f, exp_b, sf0, sb0, nb, length):
    nc = length // SSD_CHUNK
    rows = xa.shape[0]
    q = SSD_CHUNK
    sdims = (SSD_GROUPS, SSD_STATE, SSD_GW)
    state_shape = jax.ShapeDtypeStruct((nb,) + sdims, F32)
    state_spec = pl.BlockSpec((1,) + sdims, lambda b, t: (b, 0, 0, 0))
    enter_shape = jax.ShapeDtypeStruct((nb * nc,) + sdims, BF16)
    const = lambda a: pl.BlockSpec(a.shape, lambda b, t: (0,) * a.ndim)
    bcol = SSD_INNER // SSD_BC
    scratch = [pltpu.VMEM(sdims, F32)]

    rev = lambda b, t: b * nc + nc - 1 - t
    enter, s_b = pl.pallas_call(
        functools.partial(_ssd_bwd_kernel, nc),
        out_shape=[enter_shape, state_shape],
        grid=(nb, nc),
        in_specs=[pl.BlockSpec((q, SSD_INNER), lambda b, t: (rev(b, t), 0)),
                  pl.BlockSpec((q, SSD_BC), lambda b, t: (rev(b, t), bcol)),
                  pl.BlockSpec((q, LANES), lambda b, t: (rev(b, t), 0)),
                  const(alog), const(exp_b), state_spec],
        out_specs=[pl.BlockSpec((1,) + sdims, lambda b, t: (rev(b, t), 0, 0, 0)), state_spec],
        scratch_shapes=scratch,
        compiler_params=_cparams(("parallel", "arbitrary")),
        name="ssd_bwd_states",
    )(xa, xa, dt, alog, exp_b, sb0)

    fwd = lambda b, t: b * nc + t
    y, s_f = pl.pallas_call(
        functools.partial(_ssd_fwd_kernel, nc),
        out_shape=[jax.ShapeDtypeStruct((rows, SSD_INNER), BF16), state_shape],
        grid=(nb, nc),
        in_specs=[pl.BlockSpec((q, SSD_INNER), lambda b, t: (fwd(b, t), 0)),
                  pl.BlockSpec((q, SSD_BC), lambda b, t: (fwd(b, t), bcol)),
                  pl.BlockSpec((q, SSD_BC), lambda b, t: (fwd(b, t), bcol + 1)),
                  pl.BlockSpec((q, LANES), lambda b, t: (fwd(b, t), 0)),
                  const(alog), const(dsk), const(exp_f), const(exp_b), state_spec,
                  pl.BlockSpec((1,) + sdims, lambda b, t: (fwd(b, t), 0, 0, 0))],
        out_specs=[pl.BlockSpec((q, SSD_INNER), lambda b, t: (fwd(b, t), 0)), state_spec],
        scratch_shapes=scratch,
        compiler_params=_cparams(("parallel", "arbitrary")),
        name="ssd_scan",
    )(xa, xa, xa, dt, alog, dsk, exp_f, exp_b, sf0, enter)
    return y, s_f, s_b


def _fnet_kernel(n1, u_ref, ctab_ref, g_ref, f2_ref, o_ref, zr_s, zi_s, ar_s, ai_s):
    n2 = F_GROUP_DIM
    pz = n2 + FNET_PAD
    po = n1 + FNET_PAD
    slabs = min(FNET_SLABS, n1)
    out_s = zr_s

    def stage0(i, carry):
        pq = _dot(u_ref[pl.ds(pl.multiple_of(i * slabs * n2, n2), slabs * n2), :], ctab_ref[...])
        for j in range(slabs):
            dst = pl.multiple_of((i * slabs + j) * pz, 8)
            zr_s[pl.ds(dst, n2), :] = pq[j * n2:(j + 1) * n2, :n2]
            zi_s[pl.ds(dst, n2), :] = pq[j * n2:(j + 1) * n2, n2:]
        return carry

    lax.fori_loop(0, n1 // slabs, stage0, 0)

    def stage1(b, carry):
        z = jnp.concatenate([zr_s[pl.ds(b, n1, stride=pz), :],
                             zi_s[pl.ds(b, n1, stride=pz), :]], axis=0).astype(BF16)
        a = _dot(g_ref[b], z)
        ar_s[pl.ds(b, n1, stride=pz), :] = a[:n1]
        ai_s[pl.ds(b, n1, stride=pz), :] = a[n1:]
        return carry

    lax.fori_loop(0, n2, stage1, 0, unroll=FNET_UNROLL)

    def stage2(k1, carry):
        r0 = pl.multiple_of(k1 * pz, 8)
        a = jnp.concatenate([ar_s[pl.ds(r0, n2), :], ai_s[pl.ds(r0, n2), :]], axis=0).astype(BF16)
        out_s[pl.ds(k1, n2, stride=po), :] = _dot(f2_ref[...], a)
        return carry

    lax.fori_loop(0, n1, stage2, 0, unroll=FNET_UNROLL)

    def stage3(k2, carry):
        src = pl.multiple_of(k2 * po, 8)
        dst = pl.multiple_of(k2 * n1, n1)
        o_ref[pl.ds(dst, n1), :] = out_s[pl.ds(src, n1), :].astype(o_ref.dtype)
        return carry

    lax.fori_loop(0, n2, stage3, 0, unroll=FNET_UNROLL)


def _fnet_tables(length):
    n2 = F_GROUP_DIM
    n1 = length // n2
    j = np.arange(n2, dtype=np.int64)
    ang_c = 2.0 * np.pi * ((j[:, None] * j[None, :]) % n2) / n2
    ctab = np.concatenate([np.cos(ang_c), -np.sin(ang_c)], axis=1)
    k1 = np.arange(n1, dtype=np.int64)[None, :, None]
    a = np.arange(n1, dtype=np.int64)[None, None, :]
    b = np.arange(n2, dtype=np.int64)[:, None, None]
    ang = 2.0 * np.pi * ((k1 * (n2 * a + b)) % length) / length
    cs, sn = np.cos(ang), np.sin(ang)
    gtab = np.concatenate([np.concatenate([cs, sn], axis=2),
                           np.concatenate([-sn, cs], axis=2)], axis=1)
    scale = 1.0 / np.sqrt(float(length * n2))
    f2 = np.concatenate([np.cos(ang_c), np.sin(ang_c)], axis=1) * scale
    as_bf16 = lambda t: jnp.asarray(t.astype(np.float32)).astype(BF16)
    return as_bf16(ctab), as_bf16(gtab), as_bf16(f2)


def _fnet(u, nb, length):
    n2 = F_GROUP_DIM
    n1 = length // n2
    ctab, gtab, f2 = _fnet_tables(length)
    rows = max(n1 * (n2 + FNET_PAD), n2 * (n1 + FNET_PAD))
    scr = pltpu.VMEM((rows, n2), F32)
    return pl.pallas_call(
        functools.partial(_fnet_kernel, n1),
        out_shape=jax.ShapeDtypeStruct((nb * length, F_GROUPS * n2), BF16),
        grid=(nb, F_GROUPS),
        in_specs=[pl.BlockSpec((length, n2), lambda b, g: (b, g)),
                  pl.BlockSpec((n2, 2 * n2), lambda b, g: (0, 0)),
                  pl.BlockSpec((n2, 2 * n1, 2 * n1), lambda b, g: (0, 0, 0),
                               pipeline_mode=pl.Buffered(1)),
                  pl.BlockSpec((n2, 2 * n2), lambda b, g: (0, 0))],
        out_specs=pl.BlockSpec((length, n2), lambda b, g: (b, g)),
        scratch_shapes=[scr, scr, scr, scr],
        compiler_params=_cparams(("parallel", "parallel")),
        name="fnet",
    )(u, ctab, gtab, f2)


def _merge_kernel(x_ref, mix_ref, y_ref, zs_ref, sgf_ref, sgs_ref, wfa_ref, wsb_ref, wo_ref,
                  ng_ref, gate_ref, n2g_ref, sh_ref, sc_ref, lat_ref, h2_ref):
    branch_f = _dot(mix_ref[...], wfa_ref[...])
    parts = []
    for g in range(SSD_GROUPS):
        sl = slice(g * SSD_GW, (g + 1) * SSD_GW)
        v = y_ref[:, sl].astype(F32) * zs_ref[:, sl].astype(F32)
        v = v * lax.rsqrt(jnp.mean(v * v, axis=-1, keepdims=True) + EPS)
        parts.append((v * ng_ref[:, sl]).astype(BF16))
    branch_s = _dot(jnp.concatenate(parts, axis=1), wsb_ref[...])
    merged = sgf_ref[...].astype(F32) * branch_f + sgs_ref[...].astype(F32) * branch_s
    lat = x_ref[...] + gate_ref[0] * _dot(merged.astype(BF16), wo_ref[...])
    lat_ref[...] = lat
    h2_ref[...] = _rms_modulate(lat, n2g_ref[...], sh_ref[0], sc_ref[0]).astype(h2_ref.dtype)


def _merge(x2, mixed, y, zs, sgf, sgs, wfa, wsb, wo, ng, gate, n2g, shift, scale, rows_per_batch):
    rows = x2.shape[0]
    tm = min(IN_TM, rows_per_batch)
    tpb = rows_per_batch // tm
    row_spec = lambda w: pl.BlockSpec((tm, w), lambda i: (i, 0))
    full = lambda a: pl.BlockSpec(a.shape, lambda i: (0,) * a.ndim, pipeline_mode=pl.Buffered(1))
    bspec = pl.BlockSpec((1, 1, D_MODEL), lambda i: (i // tpb, 0, 0))
    return pl.pallas_call(
        _merge_kernel,
        out_shape=[jax.ShapeDtypeStruct((rows, D_MODEL), F32),
                   jax.ShapeDtypeStruct((rows, D_MODEL), BF16)],
        grid=(rows // tm,),
        in_specs=[row_spec(D_MODEL), row_spec(D_MODEL), row_spec(SSD_INNER), row_spec(SSD_INNER),
                  row_spec(D_MODEL), row_spec(D_MODEL), full(wfa), full(wsb), full(wo),
                  pl.BlockSpec((1, SSD_INNER), lambda i: (0, 0)), bspec,
                  pl.BlockSpec((1, D_MODEL), lambda i: (0, 0)), bspec, bspec],
        out_specs=[row_spec(D_MODEL), row_spec(D_MODEL)],
        compiler_params=_cparams(("parallel",)),
        name="merge",
    )(x2, mixed, y, zs, sgf, sgs, wfa, wsb, wo, ng, gate, n2g, shift, scale)


def _ffn_kernel(tiles_per_img, hm_ref, hp_ref, hn_ref, wg_ref, wv_ref, cwg_ref, cwv_ref,
                cbg_ref, cbv_ref, wd_ref, lat_ref, gate_ref, fg_ref, o_ref, hext_s, acc_s):
    i = pl.program_id(0)
    j = pl.program_id(1)
    tm = hm_ref.shape[0]
    ext = tm + 2 * GRID_W

    @pl.when(j == 0)
    def _():
        first = (i % tiles_per_img) == 0
        last = (i % tiles_per_img) == tiles_per_img - 1
        hp = hp_ref[...]
        hn = hn_ref[...]
        hext_s[0:GRID_W, :] = jnp.where(first, jnp.zeros_like(hp), hp)
        hext_s[GRID_W:GRID_W + tm, :] = hm_ref[...]
        hext_s[GRID_W + tm:ext, :] = jnp.where(last, jnp.zeros_like(hn), hn)
        acc_s[...] = jnp.zeros_like(acc_s)

    he = hext_s[...]
    nrow = tm // GRID_W
    tiles = GRID_W // F32_SUBLANES
    sub = lax.broadcasted_iota(jnp.int32, (nrow + 2, tiles, F32_SUBLANES, FFN_CW), 2)
    zero_tile = jnp.zeros((nrow + 2, 1, F32_SUBLANES, FFN_CW), F32)

    def branch(w_ref, cw_ref, cb_ref):
        u = _dot(he, w_ref[0]).reshape(nrow + 2, tiles, F32_SUBLANES, FFN_CW)
        rl = pltpu.roll(u, 1, 2)
        ul = jnp.where(sub == 0, jnp.concatenate([zero_tile, rl[:, :-1]], axis=1), rl)
        rr = pltpu.roll(u, F32_SUBLANES - 1, 2)
        ur = jnp.where(sub == F32_SUBLANES - 1, jnp.concatenate([rr[:, 1:], zero_tile], axis=1), rr)
        cw = cw_ref[0]
        out = cb_ref[0]
        for ky in range(3):
            s = slice(ky, ky + nrow)
            out = out + ul[s] * cw[3 * ky:3 * ky + 1] + u[s] * cw[3 * ky + 1:3 * ky + 2] \
                + ur[s] * cw[3 * ky + 2:3 * ky + 3]
        return out.reshape(tm, FFN_CW)

    gate = branch(wg_ref, cwg_ref, cbg_ref)
    val = branch(wv_ref, cwv_ref, cbv_ref)
    act = (gate * _sigmoid(gate) * val).astype(BF16)
    acc_s[...] += _dot(act, wd_ref[0])

    @pl.when(j == pl.num_programs(1) - 1)
    def _():
        lat = lat_ref[...] + gate_ref[0] * acc_s[...]
        ms = jnp.mean(lat * lat, axis=-1, keepdims=True)
        o_ref[...] = (lat * lax.rsqrt(ms + EPS)) * fg_ref[...]


def _conv_ffn(h2, wup3, cw3, cb3, wd3, lat1, gate2, fg, rows_per_img):
    rows = h2.shape[0]
    tm = min(FFN_TM, rows_per_img)
    tiles_per_img = rows_per_img // tm
    nj = D_FF // FFN_CW
    hb = tm // GRID_W
    nhb = rows // GRID_W
    return pl.pallas_call(
        functools.partial(_ffn_kernel, tiles_per_img),
        out_shape=jax.ShapeDtypeStruct((rows, D_MODEL), F32),
        grid=(rows // tm, nj),
        in_specs=[pl.BlockSpec((tm, D_MODEL), lambda i, j: (i, 0)),
                  pl.BlockSpec((GRID_W, D_MODEL), lambda i, j: (jnp.maximum(i * hb - 1, 0), 0)),
                  pl.BlockSpec((GRID_W, D_MODEL), lambda i, j: (jnp.minimum((i + 1) * hb, nhb - 1), 0)),
                  pl.BlockSpec((1, D_MODEL, FFN_CW), lambda i, j: (j, 0, 0)),
                  pl.BlockSpec((1, D_MODEL, FFN_CW), lambda i, j: (nj + j, 0, 0)),
                  pl.BlockSpec((1, 9, FFN_CW), lambda i, j: (j, 0, 0)),
                  pl.BlockSpec((1, 9, FFN_CW), lambda i, j: (nj + j, 0, 0)),
                  pl.BlockSpec((1, 1, FFN_CW), lambda i, j: (j, 0, 0)),
                  pl.BlockSpec((1, 1, FFN_CW), lambda i, j: (nj + j, 0, 0)),
                  pl.BlockSpec((1, FFN_CW, D_MODEL), lambda i, j: (j, 0, 0)),
                  pl.BlockSpec((tm, D_MODEL), lambda i, j: (i, 0)),
                  pl.BlockSpec((1, 1, D_MODEL), lambda i, j: (i // tiles_per_img, 0, 0)),
                  pl.BlockSpec((1, D_MODEL), lambda i, j: (0, 0))],
        out_specs=pl.BlockSpec((tm, D_MODEL), lambda i, j: (i, 0)),
        scratch_shapes=[pltpu.VMEM((tm + 2 * GRID_W, D_MODEL), BF16),
                        pltpu.VMEM((tm, D_MODEL), F32)],
        compiler_params=_cparams(("parallel", "arbitrary")),
        name="conv_ffn",
    )(h2, h2, h2, wup3, wup3, cw3, cw3, cb3, cb3, wd3, lat1, gate2, fg)


def _dt_perm():
    idx = np.arange(2 * SSD_HEADS).reshape(2, SSD_GROUPS, SSD_HPG)
    return idx.transpose(1, 0, 2).reshape(-1)


def _expansion_matrices():
    mats = []
    for d in range(2):
        e = np.zeros((LANES, SSD_INNER), np.float32)
        for g in range(SSD_GROUPS):
            for h in range(SSD_HPG):
                c0 = g * SSD_GW + h * SSD_HEAD_DIM
                e[(2 * g + d) * SSD_HPG + h, c0:c0 + SSD_HEAD_DIM] = 1.0
        mats.append(jnp.asarray(e).astype(BF16))
    return mats


def kernel(x, c, ctx, c_ctx, w_mod, b_mod, norm1_g, w_in, conv_ssd_w, conv_ssd_b, dt_bias, a_log,
           d_skip, ssd_norm_g, w_fa, w_sb, w_o, norm2_g, w_up, conv_ffn_w, conv_ffn_b, w_down, final_g):
    assert w_mod.shape[0] == 1, "single-layer block"
    nb, length, _ = x.shape
    clen = ctx.shape[1]
    assert length % FFN_TM == 0 and clen % SSD_CHUNK == 0

    cond = jnp.zeros((8, D_MODEL), F32).at[:nb].set(c).at[nb].set(c_ctx)
    mods = _adaln(cond, w_mod[0], b_mod[0][None, :])
    m_lat = [mods[:nb, k * D_MODEL:(k + 1) * D_MODEL][:, None, :] for k in range(N_MOD)]
    m_ctx = [mods[nb:nb + 1, k * D_MODEL:(k + 1) * D_MODEL][:, None, :] for k in range(2)]

    w = w_in[0]
    o_f, o_xbc, o_z = D_MODEL, D_MODEL + XBC_WIDTH, D_MODEL + XBC_WIDTH + SSD_INNER
    o_dt = o_z + 2 * SSD_HEADS
    perm = _dt_perm()
    w_dt = jnp.pad(w[:, o_z:o_dt][:, perm], ((0, 0), (0, LANES - 2 * SSD_HEADS)))
    w_lat = jnp.concatenate([w[:, :o_f], w[:, o_f:o_xbc], w[:, o_xbc:o_z], w[:, o_dt:o_dt + D_MODEL],
                             w[:, o_dt + D_MODEL:], w_dt], axis=1).astype(BF16)
    w_ctx = jnp.concatenate([w[:, o_f:o_xbc], w_dt], axis=1).astype(BF16)
    dtb = jnp.pad(dt_bias[0].reshape(-1)[perm], (0, LANES - 2 * SSD_HEADS))[None, :]
    g1 = norm1_g[0][None, :]

    x2 = x.reshape(nb * length, D_MODEL)
    segs_lat = (("id", D_MODEL), ("id", XBC_WIDTH), ("silu", SSD_INNER), ("sigmoid", D_MODEL),
                ("sigmoid", D_MODEL), ("softplus", LANES))
    u_f, xbc_l, zs, sgf, sgs, dt_l = _inproj(
        x2, m_lat[0], m_lat[1], g1, w_lat, dtb, segs_lat, (BF16, BF16, BF16, BF16, BF16, F32), length)
    segs_ctx = (("id", XBC_WIDTH), ("softplus", LANES))
    xbc_c, dt_c = _inproj(ctx.reshape(nb * clen, D_MODEL), m_ctx[0], m_ctx[1], g1, w_ctx, dtb,
                          segs_ctx, (BF16, F32), clen)

    cw, cb = conv_ssd_w[0], conv_ssd_b[0][None, :]
    alog = jnp.pad(a_log[0].reshape(-1)[perm], (0, LANES - 2 * SSD_HEADS))[None, :]
    dsk = jnp.repeat(d_skip[0], SSD_HEAD_DIM)[None, :]
    exp_f, exp_b = _expansion_matrices()
    zeros = jnp.zeros((nb, SSD_GROUPS, SSD_STATE, SSD_GW), F32)
    _, s_f, s_b = _ssd_scan(_ssd_conv(xbc_c, cw, cb, clen), dt_c, alog, dsk, exp_f, exp_b, zeros, zeros,
                            nb, clen)
    y, _, _ = _ssd_scan(_ssd_conv(xbc_l, cw, cb, length), dt_l, alog, dsk, exp_f, exp_b, s_f, s_b,
                        nb, length)

    mixed = _fnet(u_f, nb, length)

    lat1, h2 = _merge(x2, mixed, y, zs, sgf, sgs, w_fa[0].astype(BF16), w_sb[0].astype(BF16),
                      w_o[0].astype(BF16), ssd_norm_g[0][None, :], m_lat[2], norm2_g[0][None, :],
                      m_lat[3], m_lat[4], length)

    nj = D_FF // FFN_CW
    wup3 = w_up[0].astype(BF16).reshape(D_MODEL, 2 * nj, FFN_CW).transpose(1, 0, 2)
    cw3 = conv_ffn_w[0].reshape(9, 2 * nj, FFN_CW).transpose(1, 0, 2)
    cb3 = conv_ffn_b[0].reshape(2 * nj, 1, FFN_CW)
    wd3 = w_down[0].astype(BF16).reshape(nj, FFN_CW, D_MODEL)
    out = _conv_ffn(h2, wup3, cw3, cb3, wd3, lat1, m_lat[5], final_g[None, :], length)
    return out.reshape(nb, length, D_MODEL)
```

```python
import functools

import numpy as np
import jax
import jax.numpy as jnp
from jax import lax
from jax.experimental import pallas as pl
from jax.experimental.pallas import tpu as pltpu

F32 = jnp.float32
BF16 = jnp.bfloat16

D_MODEL = 1024
N_MOD = 6
EPS = 1e-6
F_GROUPS = 8
F_GROUP_DIM = 128
SSD_HEADS = 32
SSD_HEAD_DIM = 64
SSD_INNER = SSD_HEADS * SSD_HEAD_DIM
SSD_GROUPS = 4
SSD_HPG = SSD_HEADS // SSD_GROUPS
SSD_STATE = 128
SSD_BC = SSD_GROUPS * SSD_STATE
SSD_CHUNK = 128
SSD_GW = SSD_HPG * SSD_HEAD_DIM
XBC_WIDTH = SSD_INNER + 2 * SSD_BC
D_FF = 2816
GRID_W = 64
LOG2E = 1.4426950408889634

LANES = 128
F32_SUBLANES = 8
VMEM_LIMIT = 56 * 1024 * 1024

IN_TM = 512
IN_CHUNK = 512
FFN_TM = 1024
FFN_CW = 256
FFN_HALVES = 2
FNET_SLABS = 8
FNET_PAD = 8
FNET_UNROLL = 8


def _cparams(sem):
    return pltpu.CompilerParams(dimension_semantics=sem, vmem_limit_bytes=VMEM_LIMIT)


def _sigmoid(v):
    return 1.0 / (1.0 + jnp.exp(-v))


def _dot(a, b):
    return jnp.dot(a, b, preferred_element_type=F32)


def _split2(v):
    hi = v.astype(BF16)
    lo = (v - hi.astype(F32)).astype(BF16)
    return hi, lo


def _split3(v):
    hi = v.astype(BF16)
    r = v - hi.astype(F32)
    mid = r.astype(BF16)
    lo = (r - mid.astype(F32)).astype(BF16)
    return hi, mid, lo


def _mod_kernel(c_ref, w_ref, b_ref, o_ref):
    c = c_ref[...]
    s = c * _sigmoid(c)
    w = w_ref[...]
    s_hi, s_lo = _split2(s)
    w_hi, w_lo = _split2(w)
    o_ref[...] = _dot(s_hi, w_hi) + _dot(s_lo, w_hi) + _dot(s_hi, w_lo) + b_ref[...]


def _adaln(cond8, w_mod, b_mod):
    n = w_mod.shape[1]
    tn = 1024
    return pl.pallas_call(
        _mod_kernel,
        out_shape=jax.ShapeDtypeStruct((8, n), F32),
        grid=(n // tn,),
        in_specs=[pl.BlockSpec((8, D_MODEL), lambda j: (0, 0)),
                  pl.BlockSpec((D_MODEL, tn), lambda j: (0, j)),
                  pl.BlockSpec((1, tn), lambda j: (0, j))],
        out_specs=pl.BlockSpec((8, tn), lambda j: (0, j)),
        compiler_params=_cparams(("parallel",)),
        name="adaln_mod",
    )(cond8, w_mod, b_mod)


def _rms_modulate(x, g, shift, scale):
    ms = jnp.mean(x * x, axis=-1, keepdims=True)
    h = (x * lax.rsqrt(ms + EPS)) * g
    return h * (1.0 + scale) + shift


def _inproj_kernel(segs, tiles_per_seq, x_ref, xp_ref, xn_ref, sh_ref, sc_ref, g_ref, w_ref, dtb_ref,
                   cw_ref, cb_ref, *o_refs):
    i = pl.program_id(0)
    first = (i % tiles_per_seq) == 0
    last = (i % tiles_per_seq) == tiles_per_seq - 1
    tm = x_ref.shape[0]
    nt = tm // F32_SUBLANES
    g, shift, scale = g_ref[...], sh_ref[0], sc_ref[0]
    hb = _rms_modulate(x_ref[...], g, shift, scale).astype(BF16)
    halo = jnp.concatenate([xp_ref[...], xn_ref[...]], axis=0)
    hh = _rms_modulate(halo, g, shift, scale).astype(BF16)
    work = []
    col = 0
    for (kind, width), o_ref in zip(segs, o_refs):
        work += [(kind, o_ref, col + c0, c0, min(IN_CHUNK, width - c0)) for c0 in range(0, width, IN_CHUNK)]
        col += width
    for kind, o_ref, wcol, c0, cw in work:
        wc = w_ref[:, wcol:wcol + cw]
        r = _dot(hb, wc)
        if kind == "silu":
            r = r * _sigmoid(r)
        elif kind == "sigmoid":
            r = _sigmoid(r)
        elif kind == "softplus":
            r = r + dtb_ref[...]
            r = jnp.maximum(r, 0.0) + jnp.log(1.0 + jnp.exp(-jnp.abs(r)))
        elif kind == "conv_silu":
            rh = _dot(hh, wc)
            prev = jnp.where(first, 0.0, rh[F32_SUBLANES - 1:F32_SUBLANES])
            nxt = jnp.where(last, 0.0, rh[F32_SUBLANES:F32_SUBLANES + 1])
            prev = jnp.broadcast_to(prev[None], (1, F32_SUBLANES, cw))
            nxt = jnp.broadcast_to(nxt[None], (1, F32_SUBLANES, cw))
            sub = lax.broadcasted_iota(jnp.int32, (nt, F32_SUBLANES, cw), 1)
            x = r.reshape(nt, F32_SUBLANES, cw)
            rl = pltpu.roll(x, 1, 1)
            xl = jnp.where(sub == 0, jnp.concatenate([prev, rl[:-1]], axis=0), rl)
            rr = pltpu.roll(x, F32_SUBLANES - 1, 1)
            xr = jnp.where(sub == F32_SUBLANES - 1, jnp.concatenate([rr[1:], nxt], axis=0), rr)
            k = cw_ref[:, c0:c0 + cw]
            y = xl * k[0:1] + x * k[1:2] + xr * k[2:3] + cb_ref[:, c0:c0 + cw]
            r = (y * _sigmoid(y)).reshape(tm, cw)
        o_ref[:, c0:c0 + cw] = r.astype(o_ref.dtype)


def _inproj(x2, shift, scale, g, w, dtb, cw, cb, segs, out_dtypes, rows_per_batch):
    rows = x2.shape[0]
    tm = min(IN_TM, rows_per_batch)
    tiles_per_batch = rows_per_batch // tm
    nb = shift.shape[0]
    bidx = (lambda i: (i // tiles_per_batch, 0, 0)) if nb > 1 else (lambda i: (0, 0, 0))
    wtot = w.shape[1]
    hb = tm // F32_SUBLANES
    nhb = rows // F32_SUBLANES
    return pl.pallas_call(
        functools.partial(_inproj_kernel, segs, tiles_per_batch),
        out_shape=[jax.ShapeDtypeStruct((rows, wd), dt) for (_, wd), dt in zip(segs, out_dtypes)],
        grid=(rows // tm,),
        in_specs=[pl.BlockSpec((tm, D_MODEL), lambda i: (i, 0)),
                  pl.BlockSpec((F32_SUBLANES, D_MODEL), lambda i: (jnp.maximum(i * hb - 1, 0), 0)),
                  pl.BlockSpec((F32_SUBLANES, D_MODEL), lambda i: (jnp.minimum((i + 1) * hb, nhb - 1), 0)),
                  pl.BlockSpec((1, 1, D_MODEL), bidx),
                  pl.BlockSpec((1, 1, D_MODEL), bidx),
                  pl.BlockSpec((1, D_MODEL), lambda i: (0, 0)),
                  pl.BlockSpec((D_MODEL, wtot), lambda i: (0, 0), pipeline_mode=pl.Buffered(1)),
                  pl.BlockSpec((1, LANES), lambda i: (0, 0)),
                  pl.BlockSpec((3, XBC_WIDTH), lambda i: (0, 0)),
                  pl.BlockSpec((1, XBC_WIDTH), lambda i: (0, 0))],
        out_specs=[pl.BlockSpec((tm, wd), lambda i: (i, 0)) for (_, wd) in segs],
        compiler_params=_cparams(("parallel",)),
        name="inproj",
    )(x2, x2, x2, shift, scale, g, w, dtb, cw, cb)


def _expand(v, e_ref):
    lane = lax.broadcasted_iota(jnp.int32, v.shape, 1)
    v = jnp.where(lane < 2 * SSD_HEADS, v, 0.0)
    hi = v.astype(BF16).astype(F32)
    stacked = hi + pltpu.roll(v - hi, 2 * SSD_HEADS, 1)
    return _dot(stacked.astype(BF16), e_ref[...])


def _chunk_decays(dt_ref, alog_ref):
    q = SSD_CHUNK
    row = lax.broadcasted_iota(jnp.int32, (q, LANES), 0)
    lane = lax.broadcasted_iota(jnp.int32, (q, LANES), 1)
    dtv = jnp.where(lane < 2 * SSD_HEADS, dt_ref[...], 0.0)
    la = dtv * (-jnp.exp(alog_ref[...]) * LOG2E)
    tri = jnp.where(row >= lane, 1.0, 0.0).astype(BF16)
    la_hi, la_mid, la_lo = _split3(la)
    cum = _dot(tri, la_hi) + _dot(tri, la_mid) + _dot(tri, la_lo)
    excl = cum - la
    tot = cum[q - 1:q, :]
    is_f = (lane & SSD_HPG) == 0
    e_in = jnp.exp2(jnp.where(is_f, cum, tot - excl))
    w_out = jnp.exp2(jnp.where(is_f, tot - cum, excl)) * dtv
    return dtv, cum, excl, e_in, w_out


def _local_state(bm, xw):
    return lax.dot_general(bm, xw, (((0,), (0,)), ((), ())), preferred_element_type=F32)


def _ssd_bwd_kernel(nc, x_ref, b_ref, dt_ref, alog_ref, eb_ref, sb0_ref, enter_ref, sbo_ref, sb_s):
    t = pl.program_id(1)

    @pl.when(t == 0)
    def _():
        sb_s[...] = sb0_ref[0]

    enter_ref[0] = sb_s[...].astype(BF16)
    _, _, _, e_in, w_out = _chunk_decays(dt_ref, alog_ref)
    w_b = _expand(w_out, eb_ref)
    dec = _expand(e_in[0:8], eb_ref)[0:1]
    xw = (x_ref[...].astype(F32) * w_b).astype(BF16)
    for g in range(SSD_GROUPS):
        sl = slice(g * SSD_GW, (g + 1) * SSD_GW)
        loc = _local_state(b_ref[:, g * SSD_STATE:(g + 1) * SSD_STATE], xw[:, sl])
        sb_s[g] = dec[:, sl] * sb_s[g] + loc

    @pl.when(t == nc - 1)
    def _():
        sbo_ref[0] = sb_s[...]


def _ssd_fwd_kernel(nc, x_ref, b_ref, c_ref, dt_ref, alog_ref, dsk_ref, ef_ref, eb_ref, sf0_ref,
                    enter_ref, y_ref, sfo_ref, sf_s):
    t = pl.program_id(1)
    q = SSD_CHUNK

    @pl.when(t == 0)
    def _():
        sf_s[...] = sf0_ref[0]

    row = lax.broadcasted_iota(jnp.int32, (q, LANES), 0)
    lane = lax.broadcasted_iota(jnp.int32, (q, LANES), 1)
    lower = row >= lane
    diag = row == lane
    dtv, cum, excl, e_in, w_out = _chunk_decays(dt_ref, alog_ref)
    cum_t = cum.T
    excl_t = excl.T
    dt_t = dtv.T
    ef_exp = _expand(e_in, ef_ref)
    eb_exp = _expand(e_in, eb_ref)
    wf_exp = _expand(w_out, ef_ref)
    for g in range(SSD_GROUPS):
        sl = slice(g * SSD_GW, (g + 1) * SSD_GW)
        bm = b_ref[:, g * SSD_STATE:(g + 1) * SSD_STATE]
        cm = c_ref[:, g * SSD_STATE:(g + 1) * SSD_STATE]
        xf = x_ref[:, sl].astype(F32)
        scores = lax.dot_general(cm, bm, (((1,), (1,)), ((), ())), preferred_element_type=F32)
        ys = []
        for j in range(SSD_HPG // 2):
            ms = []
            for h in (2 * j, 2 * j + 1):
                lf = 2 * SSD_HPG * g + h
                lb = lf + SSD_HPG
                expo = jnp.where(lower,
                                 cum[:, lf:lf + 1] - cum_t[lf:lf + 1, :],
                                 excl_t[lb:lb + 1, :] - excl[:, lb:lb + 1])
                dtf = dt_t[lf:lf + 1, :]
                dtb = dt_t[lb:lb + 1, :]
                dsel = jnp.where(diag, dtf + dtb, jnp.where(lower, dtf, dtb))
                ms.append((scores * jnp.exp2(expo) * dsel).astype(BF16))
            xp = xf[:, j * LANES:(j + 1) * LANES]
            top = jnp.where(lane < SSD_HEAD_DIM, xp, 0.0).astype(BF16)
            bot = jnp.where(lane >= SSD_HEAD_DIM, xp, 0.0).astype(BF16)
            ys.append(_dot(jnp.concatenate(ms, axis=1), jnp.concatenate([top, bot], axis=0)))
        y = jnp.concatenate(ys, axis=1)
        y = y + ef_exp[:, sl] * _dot(cm, sf_s[g].astype(BF16))
        y = y + eb_exp[:, sl] * _dot(cm, enter_ref[0, g])
        y = y + dsk_ref[:, sl] * xf
        y_ref[:, sl] = y.astype(y_ref.dtype)
        xw = (xf * wf_exp[:, sl]).astype(BF16)
        sf_s[g] = ef_exp[q - 1:q, sl] * sf_s[g] + _local_state(bm, xw)

    @pl.when(t == nc - 1)
    def _():
        sfo_ref[0] = sf_s[...]


def _ssd_scan(xa, dt, alog, dsk, exp_f, exp_b, sf0, sb0, nb, length):
    nc = length // SSD_CHUNK
    rows = xa.shape[0]
    q = SSD_CHUNK
    sdims = (SSD_GROUPS, SSD_STATE, SSD_GW)
    state_shape = jax.ShapeDtypeStruct((nb,) + sdims, F32)
    state_spec = pl.BlockSpec((1,) + sdims, lambda b, t: (b, 0, 0, 0))
    enter_shape = jax.ShapeDtypeStruct((nb * nc,) + sdims, BF16)
    const = lambda a: pl.BlockSpec(a.shape, lambda b, t: (0,) * a.ndim)
    bcol = SSD_INNER // SSD_BC
    scratch = [pltpu.VMEM(sdims, F32)]

    rev = lambda b, t: b * nc + nc - 1 - t
    enter, s_b = pl.pallas_call(
        functools.partial(_ssd_bwd_kernel, nc),
        out_shape=[enter_shape, state_shape],
        grid=(nb, nc),
        in_specs=[pl.BlockSpec((q, SSD_INNER), lambda b, t: (rev(b, t), 0)),
                  pl.BlockSpec((q, SSD_BC), lambda b, t: (rev(b, t), bcol)),
                  pl.BlockSpec((q, LANES), lambda b, t: (rev(b, t), 0)),
                  const(alog), const(exp_b), state_spec],
        out_specs=[pl.BlockSpec((1,) + sdims, lambda b, t: (rev(b, t), 0, 0, 0)), state_spec],
        scratch_shapes=scratch,
        compiler_params=_cparams(("parallel", "arbitrary")),
        name="ssd_bwd_states",
    )(xa, xa, dt, alog, exp_b, sb0)

    fwd = lambda b, t: b * nc + t
    y, s_f = pl.pallas_call(
        functools.partial(_ssd_fwd_kernel, nc),
        out_shape=[jax.ShapeDtypeStruct((rows, SSD_INNER), BF16), state_shape],
        grid=(nb, nc),
        in_specs=[pl.BlockSpec((q, SSD_INNER), lambda b, t: (fwd(b, t), 0)),
                  pl.BlockSpec((q, SSD_BC), lambda b, t: (fwd(b, t), bcol)),
                  pl.BlockSpec((q, SSD_BC), lambda b, t: (fwd(b, t), bcol + 1)),
                  pl.BlockSpec((q, LANES), lambda b, t: (fwd(b, t), 0)),
                  const(alog), const(dsk), const(exp_f), const(exp_b), state_spec,
                  pl.BlockSpec((1,) + sdims, lambda b, t: (fwd(b, t), 0, 0, 0))],
        out_specs=[pl.BlockSpec((q, SSD_INNER), lambda b, t: (fwd(b, t), 0)), state_spec],
        scratch_shapes=scratch,
        compiler_params=_cparams(("parallel", "arbitrary")),
        name="ssd_scan",
    )(xa, xa, xa, dt, alog, dsk, exp_f, exp_b, sf0, enter)
    return y, s_f, s_b


def _fnet_kernel(n1, u_ref, ctab_ref, g_ref, f2_ref, o_ref, zr_s, zi_s, ar_s, ai_s):
    n2 = F_GROUP_DIM
    pz = n2 + FNET_PAD
    po = n1 + FNET_PAD
    slabs = min(FNET_SLABS, n1)
    out_s = zr_s

    def stage0(i, carry):
        pq = _dot(u_ref[pl.ds(pl.multiple_of(i * slabs * n2, n2), slabs * n2), :], ctab_ref[...])
        for j in range(slabs):
            dst = pl.multiple_of((i * slabs + j) * pz, 8)
            zr_s[pl.ds(dst, n2), :] = pq[j * n2:(j + 1) * n2, :n2]
            zi_s[pl.ds(dst, n2), :] = pq[j * n2:(j + 1) * n2, n2:]
        return carry

    lax.fori_loop(0, n1 // slabs, stage0, 0)

    def stage1(b, carry):
        z = jnp.concatenate([zr_s[pl.ds(b, n1, stride=pz), :],
                             zi_s[pl.ds(b, n1, stride=pz), :]], axis=0).astype(BF16)
        a = _dot(g_ref[b], z)
        ar_s[pl.ds(b, n1, stride=pz), :] = a[:n1]
        ai_s[pl.ds(b, n1, stride=pz), :] = a[n1:]
        return carry

    lax.fori_loop(0, n2, stage1, 0, unroll=FNET_UNROLL)

    def stage2(k1, carry):
        r0 = pl.multiple_of(k1 * pz, 8)
        a = jnp.concatenate([ar_s[pl.ds(r0, n2), :], ai_s[pl.ds(r0, n2), :]], axis=0).astype(BF16)
        out_s[pl.ds(k1, n2, stride=po), :] = _dot(f2_ref[...], a)
        return carry

    lax.fori_loop(0, n1, stage2, 0, unroll=FNET_UNROLL)

    def stage3(k2, carry):
        src = pl.multiple_of(k2 * po, 8)
        dst = pl.multiple_of(k2 * n1, n1)
        o_ref[pl.ds(dst, n1), :] = out_s[pl.ds(src, n1), :].astype(o_ref.dtype)
        return carry

    lax.fori_loop(0, n2, stage3, 0, unroll=FNET_UNROLL)


def _fnet_tables(length):
    n2 = F_GROUP_DIM
    n1 = length // n2
    j = np.arange(n2, dtype=np.int64)
    ang_c = 2.0 * np.pi * ((j[:, None] * j[None, :]) % n2) / n2
    ctab = np.concatenate([np.cos(ang_c), -np.sin(ang_c)], axis=1)
    k1 = np.arange(n1, dtype=np.int64)[None, :, None]
    a = np.arange(n1, dtype=np.int64)[None, None, :]
    b = np.arange(n2, dtype=np.int64)[:, None, None]
    ang = 2.0 * np.pi * ((k1 * (n2 * a + b)) % length) / length
    cs, sn = np.cos(ang), np.sin(ang)
    gtab = np.concatenate([np.concatenate([cs, sn], axis=2),
                           np.concatenate([-sn, cs], axis=2)], axis=1)
    scale = 1.0 / np.sqrt(float(length * n2))
    f2 = np.concatenate([np.cos(ang_c), np.sin(ang_c)], axis=1) * scale
    as_bf16 = lambda t: jnp.asarray(t.astype(np.float32)).astype(BF16)
    return as_bf16(ctab), as_bf16(gtab), as_bf16(f2)


def _fnet(u, nb, length):
    n2 = F_GROUP_DIM
    n1 = length // n2
    ctab, gtab, f2 = _fnet_tables(length)
    rows = max(n1 * (n2 + FNET_PAD), n2 * (n1 + FNET_PAD))
    scr = pltpu.VMEM((rows, n2), F32)
    return pl.pallas_call(
        functools.partial(_fnet_kernel, n1),
        out_shape=jax.ShapeDtypeStruct((nb * length, F_GROUPS * n2), BF16),
        grid=(nb, F_GROUPS),
        in_specs=[pl.BlockSpec((length, n2), lambda b, g: (b, g)),
                  pl.BlockSpec((n2, 2 * n2), lambda b, g: (0, 0)),
                  pl.BlockSpec((n2, 2 * n1, 2 * n1), lambda b, g: (0, 0, 0),
                               pipeline_mode=pl.Buffered(1)),
                  pl.BlockSpec((n2, 2 * n2), lambda b, g: (0, 0))],
        out_specs=pl.BlockSpec((length, n2), lambda b, g: (b, g)),
        scratch_shapes=[scr, scr, scr, scr],
        compiler_params=_cparams(("parallel", "parallel")),
        name="fnet",
    )(u, ctab, gtab, f2)


def _merge_kernel(x_ref, mix_ref, y_ref, zs_ref, sgf_ref, sgs_ref, wfa_ref, wsb_ref, wo_ref,
                  ng_ref, gate_ref, n2g_ref, sh_ref, sc_ref, lat_ref, h2_ref):
    branch_f = _dot(mix_ref[...], wfa_ref[...])
    parts = []
    for g in range(SSD_GROUPS):
        sl = slice(g * SSD_GW, (g + 1) * SSD_GW)
        v = y_ref[:, sl].astype(F32) * zs_ref[:, sl].astype(F32)
        v = v * lax.rsqrt(jnp.mean(v * v, axis=-1, keepdims=True) + EPS)
        parts.append((v * ng_ref[:, sl]).astype(BF16))
    branch_s = _dot(jnp.concatenate(parts, axis=1), wsb_ref[...])
    merged = sgf_ref[...].astype(F32) * branch_f + sgs_ref[...].astype(F32) * branch_s
    lat = x_ref[...] + gate_ref[0] * _dot(merged.astype(BF16), wo_ref[...])
    lat_ref[...] = lat
    h2_ref[...] = _rms_modulate(lat, n2g_ref[...], sh_ref[0], sc_ref[0]).astype(h2_ref.dtype)


def _merge(x2, mixed, y, zs, sgf, sgs, wfa, wsb, wo, ng, gate, n2g, shift, scale, rows_per_batch):
    rows = x2.shape[0]
    tm = min(IN_TM, rows_per_batch)
    tpb = rows_per_batch // tm
    row_spec = lambda w: pl.BlockSpec((tm, w), lambda i: (i, 0))
    full = lambda a: pl.BlockSpec(a.shape, lambda i: (0,) * a.ndim, pipeline_mode=pl.Buffered(1))
    bspec = pl.BlockSpec((1, 1, D_MODEL), lambda i: (i // tpb, 0, 0))
    return pl.pallas_call(
        _merge_kernel,
        out_shape=[jax.ShapeDtypeStruct((rows, D_MODEL), F32),
                   jax.ShapeDtypeStruct((rows, D_MODEL), BF16)],
        grid=(rows // tm,),
        in_specs=[row_spec(D_MODEL), row_spec(D_MODEL), row_spec(SSD_INNER), row_spec(SSD_INNER),
                  row_spec(D_MODEL), row_spec(D_MODEL), full(wfa), full(wsb), full(wo),
                  pl.BlockSpec((1, SSD_INNER), lambda i: (0, 0)), bspec,
                  pl.BlockSpec((1, D_MODEL), lambda i: (0, 0)), bspec, bspec],
        out_specs=[row_spec(D_MODEL), row_spec(D_MODEL)],
        compiler_params=_cparams(("parallel",)),
        name="merge",
    )(x2, mixed, y, zs, sgf, sgs, wfa, wsb, wo, ng, gate, n2g, shift, scale)


def _ffn_kernel(tiles_per_img, nj, hm_ref, hp_ref, hn_ref, *refs):
    n_w = 7 * FFN_HALVES
    wrefs = [refs[7 * r:7 * r + 7] for r in range(FFN_HALVES)]
    lat_ref, gate_ref, fg_ref, o_ref, hext_s, ug_s, uv_s, acc_s = refs[n_w:]
    i = pl.program_id(0)
    m = pl.program_id(1)
    tm = hm_ref.shape[0]
    ext = tm + 2 * GRID_W
    nrow = tm // GRID_W
    tiles = GRID_W // F32_SUBLANES
    shape4 = (nrow + 2, tiles, F32_SUBLANES, FFN_CW)

    @pl.when(jnp.logical_and(i == 0, m == 0))
    def _():
        ug_s[1] = jnp.zeros((ext, FFN_CW), F32)
        uv_s[1] = jnp.zeros((ext, FFN_CW), F32)

    @pl.when(m == 0)
    def _():
        first = (i % tiles_per_img) == 0
        last = (i % tiles_per_img) == tiles_per_img - 1
        hp = hp_ref[...]
        hn = hn_ref[...]
        hext_s[0:GRID_W, :] = jnp.where(first, jnp.zeros_like(hp), hp)
        hext_s[GRID_W:GRID_W + tm, :] = hm_ref[...]
        hext_s[GRID_W + tm:ext, :] = jnp.where(last, jnp.zeros_like(hn), hn)
        acc_s[...] = jnp.zeros_like(acc_s)

    sub = lax.broadcasted_iota(jnp.int32, shape4, 2)
    zero_tile = jnp.zeros((nrow + 2, 1, F32_SUBLANES, FFN_CW), F32)

    def conv(u, cw_ref, cb_ref):
        u = u.reshape(shape4)
        rl = pltpu.roll(u, 1, 2)
        ul = jnp.where(sub == 0, jnp.concatenate([zero_tile, rl[:, :-1]], axis=1), rl)
        rr = pltpu.roll(u, F32_SUBLANES - 1, 2)
        ur = jnp.where(sub == F32_SUBLANES - 1, jnp.concatenate([rr[:, 1:], zero_tile], axis=1), rr)
        cw = cw_ref[...]
        out = cb_ref[...]
        for ky in range(3):
            r = slice(ky, ky + nrow)
            out = out + ul[r] * cw[3 * ky:3 * ky + 1] + u[r] * cw[3 * ky + 1:3 * ky + 2] \
                + ur[r] * cw[3 * ky + 2:3 * ky + 3]
        return out.reshape(tm, FFN_CW)

    he = hext_s[...]
    for r in range(FFN_HALVES):
        wg_ref, wv_ref, cwg_ref, cwv_ref, cbg_ref, cbv_ref, wd_ref = wrefs[r]
        wr, rd = r % 2, (r + 1) % 2
        ug_s[wr] = _dot(he, wg_ref[...])
        uv_s[wr] = _dot(he, wv_ref[...])
        gate = conv(ug_s[rd], cwg_ref, cbg_ref)
        val = conv(uv_s[rd], cwv_ref, cbv_ref)
        if r == 0:
            val = val * jnp.where(m == 0, 0.0, 1.0).astype(F32)
        act = (gate * _sigmoid(gate) * val).astype(BF16)
        acc_s[...] += _dot(act, wd_ref[...])

    @pl.when(m == pl.num_programs(1) - 1)
    def _():
        lat = lat_ref[...] + gate_ref[0] * acc_s[...]
        ms = jnp.mean(lat * lat, axis=-1, keepdims=True)
        o_ref[...] = (lat * lax.rsqrt(ms + EPS)) * fg_ref[...]


def _conv_ffn(h2, wup, cw, cb, wd, lat1, gate2, fg, rows_per_img):
    rows = h2.shape[0]
    tm = min(FFN_TM, rows_per_img)
    tiles_per_img = rows_per_img // tm
    nj = D_FF // FFN_CW
    steps = (nj + 1) // FFN_HALVES
    hb = tm // GRID_W
    nhb = rows // GRID_W
    ext = tm + 2 * GRID_W
    w_specs, w_args = [], []
    for r in range(FFN_HALVES):
        up = lambda m, r=r: jnp.minimum(FFN_HALVES * m + r, nj - 1)
        cv = lambda m, r=r: jnp.maximum(FFN_HALVES * m + r - 1, 0)
        w_specs += [pl.BlockSpec((D_MODEL, FFN_CW), lambda i, m, f=up: (0, f(m))),
                    pl.BlockSpec((D_MODEL, FFN_CW), lambda i, m, f=up: (0, nj + f(m))),
                    pl.BlockSpec((9, FFN_CW), lambda i, m, f=cv: (0, f(m))),
                    pl.BlockSpec((9, FFN_CW), lambda i, m, f=cv: (0, nj + f(m))),
                    pl.BlockSpec((1, FFN_CW), lambda i, m, f=cv: (0, f(m))),
                    pl.BlockSpec((1, FFN_CW), lambda i, m, f=cv: (0, nj + f(m))),
                    pl.BlockSpec((FFN_CW, D_MODEL), lambda i, m, f=cv: (f(m), 0))]
        w_args += [wup, wup, cw, cw, cb, cb, wd]
    return pl.pallas_call(
        functools.partial(_ffn_kernel, tiles_per_img, nj),
        out_shape=jax.ShapeDtypeStruct((rows, D_MODEL), F32),
        grid=(rows // tm, steps),
        in_specs=[pl.BlockSpec((tm, D_MODEL), lambda i, m: (i, 0)),
                  pl.BlockSpec((GRID_W, D_MODEL), lambda i, m: (jnp.maximum(i * hb - 1, 0), 0)),
                  pl.BlockSpec((GRID_W, D_MODEL), lambda i, m: (jnp.minimum((i + 1) * hb, nhb - 1), 0))]
                 + w_specs
                 + [pl.BlockSpec((tm, D_MODEL), lambda i, m: (i, 0)),
                    pl.BlockSpec((1, 1, D_MODEL), lambda i, m: (i // tiles_per_img, 0, 0)),
                    pl.BlockSpec((1, D_MODEL), lambda i, m: (0, 0))],
        out_specs=pl.BlockSpec((tm, D_MODEL), lambda i, m: (i, 0)),
        scratch_shapes=[pltpu.VMEM((ext, D_MODEL), BF16),
                        pltpu.VMEM((2, ext, FFN_CW), F32),
                        pltpu.VMEM((2, ext, FFN_CW), F32),
                        pltpu.VMEM((tm, D_MODEL), F32)],
        compiler_params=_cparams(("arbitrary", "arbitrary")),
        name="conv_ffn",
    )(h2, h2, h2, *w_args, lat1, gate2, fg)


def _dt_perm():
    idx = np.arange(2 * SSD_HEADS).reshape(2, SSD_GROUPS, SSD_HPG)
    return idx.transpose(1, 0, 2).reshape(-1)


def _expansion_matrices():
    mats = []
    for d in range(2):
        e = np.zeros((LANES, SSD_INNER), np.float32)
        for g in range(SSD_GROUPS):
            for h in range(SSD_HPG):
                c0 = g * SSD_GW + h * SSD_HEAD_DIM
                e[(2 * g + d) * SSD_HPG + h, c0:c0 + SSD_HEAD_DIM] = 1.0
        e[2 * SSD_HEADS:] = e[:2 * SSD_HEADS]
        mats.append(jnp.asarray(e).astype(BF16))
    return mats


def kernel(x, c, ctx, c_ctx, w_mod, b_mod, norm1_g, w_in, conv_ssd_w, conv_ssd_b, dt_bias, a_log,
           d_skip, ssd_norm_g, w_fa, w_sb, w_o, norm2_g, w_up, conv_ffn_w, conv_ffn_b, w_down, final_g):
    assert w_mod.shape[0] == 1, "single-layer block"
    nb, length, _ = x.shape
    clen = ctx.shape[1]
    assert length % FFN_TM == 0 and clen % SSD_CHUNK == 0

    cond = jnp.zeros((8, D_MODEL), F32).at[:nb].set(c).at[nb].set(c_ctx)
    mods = _adaln(cond, w_mod[0], b_mod[0][None, :])
    m_lat = [mods[:nb, k * D_MODEL:(k + 1) * D_MODEL][:, None, :] for k in range(N_MOD)]
    m_ctx = [mods[nb:nb + 1, k * D_MODEL:(k + 1) * D_MODEL][:, None, :] for k in range(2)]

    w = w_in[0]
    o_f, o_xbc, o_z = D_MODEL, D_MODEL + XBC_WIDTH, D_MODEL + XBC_WIDTH + SSD_INNER
    o_dt = o_z + 2 * SSD_HEADS
    perm = _dt_perm()
    w_dt = jnp.pad(w[:, o_z:o_dt][:, perm], ((0, 0), (0, LANES - 2 * SSD_HEADS)))
    w_lat = jnp.concatenate([w[:, :o_f], w[:, o_f:o_xbc], w[:, o_xbc:o_z], w[:, o_dt:o_dt + D_MODEL],
                             w[:, o_dt + D_MODEL:], w_dt], axis=1).astype(BF16)
    w_ctx = jnp.concatenate([w[:, o_f:o_xbc], w_dt], axis=1).astype(BF16)
    dtb = jnp.pad(dt_bias[0].reshape(-1)[perm], (0, LANES - 2 * SSD_HEADS))[None, :]
    g1 = norm1_g[0][None, :]

    x2 = x.reshape(nb * length, D_MODEL)
    cw, cb = conv_ssd_w[0], conv_ssd_b[0][None, :]
    segs_lat = (("id", D_MODEL), ("conv_silu", XBC_WIDTH), ("silu", SSD_INNER), ("sigmoid", D_MODEL),
                ("sigmoid", D_MODEL), ("softplus", LANES))
    u_f, xbc_l, zs, sgf, sgs, dt_l = _inproj(
        x2, m_lat[0], m_lat[1], g1, w_lat, dtb, cw, cb, segs_lat, (BF16, BF16, BF16, BF16, BF16, F32),
        length)
    segs_ctx = (("conv_silu", XBC_WIDTH), ("softplus", LANES))
    xbc_c, dt_c = _inproj(ctx.reshape(nb * clen, D_MODEL), m_ctx[0], m_ctx[1], g1, w_ctx, dtb, cw, cb,
                          segs_ctx, (BF16, F32), clen)

    alog = jnp.pad(a_log[0].reshape(-1)[perm], (0, LANES - 2 * SSD_HEADS))[None, :]
    dsk = jnp.repeat(d_skip[0], SSD_HEAD_DIM)[None, :]
    exp_f, exp_b = _expansion_matrices()
    zeros = jnp.zeros((nb, SSD_GROUPS, SSD_STATE, SSD_GW), F32)
    _, s_f, s_b = _ssd_scan(xbc_c, dt_c, alog, dsk, exp_f, exp_b, zeros, zeros,
                            nb, clen)
    y, _, _ = _ssd_scan(xbc_l, dt_l, alog, dsk, exp_f, exp_b, s_f, s_b,
                        nb, length)

    mixed = _fnet(u_f, nb, length)

    lat1, h2 = _merge(x2, mixed, y, zs, sgf, sgs, w_fa[0].astype(BF16), w_sb[0].astype(BF16),
                      w_o[0].astype(BF16), ssd_norm_g[0][None, :], m_lat[2], norm2_g[0][None, :],
                      m_lat[3], m_lat[4], length)

    out = _conv_ffn(h2, w_up[0].astype(BF16), conv_ffn_w[0].reshape(9, 2 * D_FF), conv_ffn_b[0][None, :],
                    w_down[0].astype(BF16), lat1, m_lat[5], final_g[None, :], length)
    return out.reshape(nb, length, D_MODEL)
```

```python
import functools

import numpy as np
import jax
import jax.numpy as jnp
from jax import lax
from jax.experimental import pallas as pl
from jax.experimental.pallas import tpu as pltpu

F32 = jnp.float32
BF16 = jnp.bfloat16

D_MODEL = 1024
N_MOD = 6
EPS = 1e-6
F_GROUPS = 8
F_GROUP_DIM = 128
SSD_HEADS = 32
SSD_HEAD_DIM = 64
SSD_INNER = SSD_HEADS * SSD_HEAD_DIM
SSD_GROUPS = 4
SSD_HPG = SSD_HEADS // SSD_GROUPS
SSD_STATE = 128
SSD_BC = SSD_GROUPS * SSD_STATE
SSD_CHUNK = 128
SSD_GW = SSD_HPG * SSD_HEAD_DIM
XBC_WIDTH = SSD_INNER + 2 * SSD_BC
D_FF = 2816
GRID_W = 64
LOG2E = 1.4426950408889634

LANES = 128
F32_SUBLANES = 8
BF16_SUBLANES = 16
VMEM_LIMIT = 56 * 1024 * 1024

IN_TM = 512
IN_CHUNK = 512
FFN_TM = 1024
FFN_CW = 256
FFN_HALVES = 2
FNET_SLABS = 8
FNET_PAD = 8
FNET_UNROLL = 32
FNET_UNROLL0 = 4


def _cparams(sem):
    return pltpu.CompilerParams(dimension_semantics=sem, vmem_limit_bytes=VMEM_LIMIT)


def _sigmoid(v):
    return 1.0 / (1.0 + jnp.exp(-v))


def _dot(a, b):
    return jnp.dot(a, b, preferred_element_type=F32)


def _split2(v):
    hi = v.astype(BF16)
    lo = (v - hi.astype(F32)).astype(BF16)
    return hi, lo


def _split3(v):
    hi = v.astype(BF16)
    r = v - hi.astype(F32)
    mid = r.astype(BF16)
    lo = (r - mid.astype(F32)).astype(BF16)
    return hi, mid, lo


def _mod_kernel(c_ref, w_ref, b_ref, o_ref):
    c = c_ref[...]
    s = c * _sigmoid(c)
    w = w_ref[...]
    s_hi, s_lo = _split2(s)
    w_hi, w_lo = _split2(w)
    o_ref[...] = _dot(s_hi, w_hi) + _dot(s_lo, w_hi) + _dot(s_hi, w_lo) + b_ref[...]


def _adaln(cond8, w_mod, b_mod):
    n = w_mod.shape[1]
    tn = 1024
    return pl.pallas_call(
        _mod_kernel,
        out_shape=jax.ShapeDtypeStruct((8, n), F32),
        grid=(n // tn,),
        in_specs=[pl.BlockSpec((8, D_MODEL), lambda j: (0, 0)),
                  pl.BlockSpec((D_MODEL, tn), lambda j: (0, j)),
                  pl.BlockSpec((1, tn), lambda j: (0, j))],
        out_specs=pl.BlockSpec((8, tn), lambda j: (0, j)),
        compiler_params=_cparams(("parallel",)),
        name="adaln_mod",
    )(cond8, w_mod, b_mod)


def _rms_modulate(x, g, shift, scale):
    ms = jnp.mean(x * x, axis=-1, keepdims=True)
    h = (x * lax.rsqrt(ms + EPS)) * g
    return h * (1.0 + scale) + shift


def _inproj_kernel(segs, tiles_per_seq, x_ref, xp_ref, xn_ref, sh_ref, sc_ref, g_ref, w_ref, dtb_ref,
                   cw_ref, cb_ref, *o_refs):
    i = pl.program_id(0)
    first = (i % tiles_per_seq) == 0
    last = (i % tiles_per_seq) == tiles_per_seq - 1
    tm = x_ref.shape[0]
    nt = tm // F32_SUBLANES
    g, shift, scale = g_ref[...], sh_ref[0], sc_ref[0]
    hb = _rms_modulate(x_ref[...], g, shift, scale).astype(BF16)
    halo = jnp.concatenate([xp_ref[...], xn_ref[...]], axis=0)
    hh = _rms_modulate(halo, g, shift, scale).astype(BF16)
    work = []
    col = 0
    for (kind, width), o_ref in zip(segs, o_refs):
        work += [(kind, o_ref, col + c0, c0, min(IN_CHUNK, width - c0)) for c0 in range(0, width, IN_CHUNK)]
        col += width
    hall = jnp.concatenate([hb, hh], axis=0)
    for kind, o_ref, wcol, c0, cw in work:
        wc = w_ref[:, wcol:wcol + cw]
        if kind == "conv_silu":
            r = _dot(hall, wc)
            r, rh = r[:tm], r[tm:]
        else:
            r = _dot(hb, wc)
        if kind == "silu":
            r = r * _sigmoid(r)
        elif kind == "sigmoid":
            r = _sigmoid(r)
        elif kind == "softplus":
            r = r + dtb_ref[...]
            r = jnp.maximum(r, 0.0) + jnp.log(1.0 + jnp.exp(-jnp.abs(r)))
        elif kind == "conv_silu":
            prev = jnp.where(first, 0.0, rh[F32_SUBLANES - 1:F32_SUBLANES])
            nxt = jnp.where(last, 0.0, rh[F32_SUBLANES:F32_SUBLANES + 1])
            prev = jnp.broadcast_to(prev[None], (1, F32_SUBLANES, cw))
            nxt = jnp.broadcast_to(nxt[None], (1, F32_SUBLANES, cw))
            sub = lax.broadcasted_iota(jnp.int32, (nt, F32_SUBLANES, cw), 1)
            x = r.reshape(nt, F32_SUBLANES, cw)
            rl = pltpu.roll(x, 1, 1)
            xl = jnp.where(sub == 0, jnp.concatenate([prev, rl[:-1]], axis=0), rl)
            rr = pltpu.roll(x, F32_SUBLANES - 1, 1)
            xr = jnp.where(sub == F32_SUBLANES - 1, jnp.concatenate([rr[1:], nxt], axis=0), rr)
            k = cw_ref[:, c0:c0 + cw]
            y = xl * k[0:1] + x * k[1:2] + xr * k[2:3] + cb_ref[:, c0:c0 + cw]
            r = (y * _sigmoid(y)).reshape(tm, cw)
        o_ref[:, c0:c0 + cw] = r.astype(o_ref.dtype)


def _inproj(x2, shift, scale, g, w, dtb, cw, cb, segs, out_dtypes, rows_per_batch):
    rows = x2.shape[0]
    tm = min(IN_TM, rows_per_batch)
    tiles_per_batch = rows_per_batch // tm
    nb = shift.shape[0]
    bidx = (lambda i: (i // tiles_per_batch, 0, 0)) if nb > 1 else (lambda i: (0, 0, 0))
    wtot = w.shape[1]
    hb = tm // F32_SUBLANES
    nhb = rows // F32_SUBLANES
    return pl.pallas_call(
        functools.partial(_inproj_kernel, segs, tiles_per_batch),
        out_shape=[jax.ShapeDtypeStruct((rows, wd), dt) for (_, wd), dt in zip(segs, out_dtypes)],
        grid=(rows // tm,),
        in_specs=[pl.BlockSpec((tm, D_MODEL), lambda i: (i, 0)),
                  pl.BlockSpec((F32_SUBLANES, D_MODEL), lambda i: (jnp.maximum(i * hb - 1, 0), 0)),
                  pl.BlockSpec((F32_SUBLANES, D_MODEL), lambda i: (jnp.minimum((i + 1) * hb, nhb - 1), 0)),
                  pl.BlockSpec((1, 1, D_MODEL), bidx),
                  pl.BlockSpec((1, 1, D_MODEL), bidx),
                  pl.BlockSpec((1, D_MODEL), lambda i: (0, 0)),
                  pl.BlockSpec((D_MODEL, wtot), lambda i: (0, 0), pipeline_mode=pl.Buffered(1)),
                  pl.BlockSpec((1, LANES), lambda i: (0, 0)),
                  pl.BlockSpec((3, XBC_WIDTH), lambda i: (0, 0)),
                  pl.BlockSpec((1, XBC_WIDTH), lambda i: (0, 0))],
        out_specs=[pl.BlockSpec((tm, wd), lambda i: (i, 0)) for (_, wd) in segs],
        compiler_params=_cparams(("parallel",)),
        name="inproj",
    )(x2, x2, x2, shift, scale, g, w, dtb, cw, cb)


def _expand(v, e_ref):
    lane = lax.broadcasted_iota(jnp.int32, v.shape, 1)
    v = jnp.where(lane < 2 * SSD_HEADS, v, 0.0)
    hi = v.astype(BF16).astype(F32)
    stacked = hi + pltpu.roll(v - hi, 2 * SSD_HEADS, 1)
    return _dot(stacked.astype(BF16), e_ref[...])


def _chunk_decays(dt_ref, alog_ref):
    q = SSD_CHUNK
    row = lax.broadcasted_iota(jnp.int32, (q, LANES), 0)
    lane = lax.broadcasted_iota(jnp.int32, (q, LANES), 1)
    dtv = jnp.where(lane < 2 * SSD_HEADS, dt_ref[...], 0.0)
    la = dtv * (-jnp.exp(alog_ref[...]) * LOG2E)
    tri = jnp.where(row >= lane, 1.0, 0.0).astype(BF16)
    la_hi, la_mid, la_lo = _split3(la)
    cum = _dot(tri, la_hi) + _dot(tri, la_mid) + _dot(tri, la_lo)
    excl = cum - la
    tot = cum[q - 1:q, :]
    is_f = (lane & SSD_HPG) == 0
    e_in = jnp.exp2(jnp.where(is_f, cum, tot - excl))
    w_out = jnp.exp2(jnp.where(is_f, tot - cum, excl)) * dtv
    return dtv, cum, excl, e_in, w_out


def _local_state(bm, xw):
    return lax.dot_general(bm, xw, (((0,), (0,)), ((), ())), preferred_element_type=F32)


def _ssd_bwd_kernel(nc, x_ref, b_ref, dt_ref, alog_ref, eb_ref, sb0_ref, enter_ref, sbo_ref, sb_s):
    t = pl.program_id(1)

    @pl.when(t == 0)
    def _():
        sb_s[...] = sb0_ref[0]

    enter_ref[0] = sb_s[...].astype(BF16)
    _, _, _, e_in, w_out = _chunk_decays(dt_ref, alog_ref)
    q = SSD_CHUNK
    both = _expand(jnp.concatenate([w_out, e_in[0:BF16_SUBLANES]], axis=0), eb_ref)
    w_b, dec = both[:q], both[q:q + 1]
    xw = (x_ref[...].astype(F32) * w_b).astype(BF16)
    for g in range(SSD_GROUPS):
        sl = slice(g * SSD_GW, (g + 1) * SSD_GW)
        loc = _local_state(b_ref[:, g * SSD_STATE:(g + 1) * SSD_STATE], xw[:, sl])
        sb_s[g] = dec[:, sl] * sb_s[g] + loc

    @pl.when(t == nc - 1)
    def _():
        sbo_ref[0] = sb_s[...]


def _ssd_fwd_kernel(nc, x_ref, b_ref, c_ref, dt_ref, alog_ref, dsk_ref, ef_ref, eb_ref, sf0_ref,
                    enter_ref, y_ref, sfo_ref, sf_s):
    t = pl.program_id(1)
    q = SSD_CHUNK

    @pl.when(t == 0)
    def _():
        sf_s[...] = sf0_ref[0]

    row = lax.broadcasted_iota(jnp.int32, (q, LANES), 0)
    lane = lax.broadcasted_iota(jnp.int32, (q, LANES), 1)
    lower = row >= lane
    diag = row == lane
    dtv, cum, excl, e_in, w_out = _chunk_decays(dt_ref, alog_ref)
    cum_t = cum.T
    excl_t = excl.T
    dt_t = dtv.T
    both = _expand(jnp.concatenate([e_in, w_out], axis=0), ef_ref)
    ef_exp, wf_exp = both[:q], both[q:]
    eb_exp = _expand(e_in, eb_ref)
    for g in range(SSD_GROUPS):
        sl = slice(g * SSD_GW, (g + 1) * SSD_GW)
        bm = b_ref[:, g * SSD_STATE:(g + 1) * SSD_STATE]
        cm = c_ref[:, g * SSD_STATE:(g + 1) * SSD_STATE]
        xf = x_ref[:, sl].astype(F32)
        scores = lax.dot_general(cm, bm, (((1,), (1,)), ((), ())), preferred_element_type=F32)
        ys = []
        for j in range(SSD_HPG // 2):
            ms = []
            for h in (2 * j, 2 * j + 1):
                lf = 2 * SSD_HPG * g + h
                lb = lf + SSD_HPG
                expo = jnp.where(lower,
                                 cum[:, lf:lf + 1] - cum_t[lf:lf + 1, :],
                                 excl_t[lb:lb + 1, :] - excl[:, lb:lb + 1])
                dtf = dt_t[lf:lf + 1, :]
                dtb = dt_t[lb:lb + 1, :]
                dsel = jnp.where(diag, dtf + dtb, jnp.where(lower, dtf, dtb))
                ms.append((scores * jnp.exp2(expo) * dsel).astype(BF16))
            xp = xf[:, j * LANES:(j + 1) * LANES]
            top = jnp.where(lane < SSD_HEAD_DIM, xp, 0.0).astype(BF16)
            bot = jnp.where(lane >= SSD_HEAD_DIM, xp, 0.0).astype(BF16)
            ys.append(_dot(jnp.concatenate(ms, axis=1), jnp.concatenate([top, bot], axis=0)))
        y = jnp.concatenate(ys, axis=1)
        y = y + ef_exp[:, sl] * _dot(cm, sf_s[g].astype(BF16))
        y = y + eb_exp[:, sl] * _dot(cm, enter_ref[0, g])
        y = y + dsk_ref[:, sl] * xf
        y_ref[:, sl] = y.astype(y_ref.dtype)
        xw = (xf * wf_exp[:, sl]).astype(BF16)
        sf_s[g] = ef_exp[q - 1:q, sl] * sf_s[g] + _local_state(bm, xw)

    @pl.when(t == nc - 1)
    def _():
        sfo_ref[0] = sf_s[...]


def _ssd_scan(xa, dt, alog, dsk, exp_f, exp_b, sf0, sb0, nb, length):
    nc = length // SSD_CHUNK
    rows = xa.shape[0]
    q = SSD_CHUNK
    sdims = (SSD_GROUPS, SSD_STATE, SSD_GW)
    state_shape = jax.ShapeDtypeStruct((nb,) + sdims, F32)
    state_spec = pl.BlockSpec((1,) + sdims, lambda b, t: (b, 0, 0, 0))
    enter_shape = jax.ShapeDtypeStruct((nb * nc,) + sdims, BF16)
    const = lambda a: pl.BlockSpec(a.shape, lambda b, t: (0,) * a.ndim)
    bcol = SSD_INNER // SSD_BC
    scratch = [pltpu.VMEM(sdims, F32)]

    rev = lambda b, t: b * nc + nc - 1 - t
    enter, s_b = pl.pallas_call(
        functools.partial(_ssd_bwd_kernel, nc),
        out_shape=[enter_shape, state_shape],
        grid=(nb, nc),
        in_specs=[pl.BlockSpec((q, SSD_INNER), lambda b, t: (rev(b, t), 0)),
                  pl.BlockSpec((q, SSD_BC), lambda b, t: (rev(b, t), bcol)),
                  pl.BlockSpec((q, LANES), lambda b, t: (rev(b, t), 0)),
                  const(alog), const(exp_b), state_spec],
        out_specs=[pl.BlockSpec((1,) + sdims, lambda b, t: (rev(b, t), 0, 0, 0)), state_spec],
        scratch_shapes=scratch,
        compiler_params=_cparams(("parallel", "arbitrary")),
        name="ssd_bwd_states",
    )(xa, xa, dt, alog, exp_b, sb0)

    fwd = lambda b, t: b * nc + t
    y, s_f = pl.pallas_call(
        functools.partial(_ssd_fwd_kernel, nc),
        out_shape=[jax.ShapeDtypeStruct((rows, SSD_INNER), BF16), state_shape],
        grid=(nb, nc),
        in_specs=[pl.BlockSpec((q, SSD_INNER), lambda b, t: (fwd(b, t), 0)),
                  pl.BlockSpec((q, SSD_BC), lambda b, t: (fwd(b, t), bcol)),
                  pl.BlockSpec((q, SSD_BC), lambda b, t: (fwd(b, t), bcol + 1)),
                  pl.BlockSpec((q, LANES), lambda b, t: (fwd(b, t), 0)),
                  const(alog), const(dsk), const(exp_f), const(exp_b), state_spec,
                  pl.BlockSpec((1,) + sdims, lambda b, t: (fwd(b, t), 0, 0, 0))],
        out_specs=[pl.BlockSpec((q, SSD_INNER), lambda b, t: (fwd(b, t), 0)), state_spec],
        scratch_shapes=scratch,
        compiler_params=_cparams(("parallel", "arbitrary")),
        name="ssd_scan",
    )(xa, xa, xa, dt, alog, dsk, exp_f, exp_b, sf0, enter)
    return y, s_f, s_b


def _fnet_kernel(n1, u_ref, ctab_ref, g_ref, f2_ref, o_ref, zr_s, zi_s, ar_s, ai_s):
    n2 = F_GROUP_DIM
    pz = n2 + FNET_PAD
    po = n1 + FNET_PAD
    slabs = min(FNET_SLABS, n1)
    out_s = zr_s

    def stage0(i, carry):
        pq = _dot(u_ref[pl.ds(pl.multiple_of(i * slabs * n2, n2), slabs * n2), :], ctab_ref[...])
        for j in range(slabs):
            dst = pl.multiple_of((i * slabs + j) * pz, 8)
            zr_s[pl.ds(dst, n2), :] = pq[j * n2:(j + 1) * n2, :n2]
            zi_s[pl.ds(dst, n2), :] = pq[j * n2:(j + 1) * n2, n2:]
        return carry

    lax.fori_loop(0, n1 // slabs, stage0, 0, unroll=min(FNET_UNROLL0, n1 // slabs))

    def stage1(b, carry):
        z = jnp.concatenate([zr_s[pl.ds(b, n1, stride=pz), :],
                             zi_s[pl.ds(b, n1, stride=pz), :]], axis=0).astype(BF16)
        a = _dot(g_ref[b], z)
        ar_s[pl.ds(b, n1, stride=pz), :] = a[:n1]
        ai_s[pl.ds(b, n1, stride=pz), :] = a[n1:]
        return carry

    lax.fori_loop(0, n2, stage1, 0, unroll=min(FNET_UNROLL, n2))

    def stage2(k1, carry):
        r0 = pl.multiple_of(k1 * pz, 8)
        a = jnp.concatenate([ar_s[pl.ds(r0, n2), :], ai_s[pl.ds(r0, n2), :]], axis=0).astype(BF16)
        out_s[pl.ds(k1, n2, stride=po), :] = _dot(f2_ref[...], a)
        return carry

    lax.fori_loop(0, n1, stage2, 0, unroll=min(FNET_UNROLL, n1))

    def stage3(k2, carry):
        src = pl.multiple_of(k2 * po, 8)
        dst = pl.multiple_of(k2 * n1, n1)
        o_ref[pl.ds(dst, n1), :] = out_s[pl.ds(src, n1), :].astype(o_ref.dtype)
        return carry

    lax.fori_loop(0, n2, stage3, 0, unroll=min(FNET_UNROLL, n2))


def _fnet_tables(length):
    n2 = F_GROUP_DIM
    n1 = length // n2
    j = np.arange(n2, dtype=np.int64)
    ang_c = 2.0 * np.pi * ((j[:, None] * j[None, :]) % n2) / n2
    ctab = np.concatenate([np.cos(ang_c), -np.sin(ang_c)], axis=1)
    k1 = np.arange(n1, dtype=np.int64)[None, :, None]
    a = np.arange(n1, dtype=np.int64)[None, None, :]
    b = np.arange(n2, dtype=np.int64)[:, None, None]
    ang = 2.0 * np.pi * ((k1 * (n2 * a + b)) % length) / length
    cs, sn = np.cos(ang), np.sin(ang)
    gtab = np.concatenate([np.concatenate([cs, sn], axis=2),
                           np.concatenate([-sn, cs], axis=2)], axis=1)
    scale = 1.0 / np.sqrt(float(length * n2))
    f2 = np.concatenate([np.cos(ang_c), np.sin(ang_c)], axis=1) * scale
    as_bf16 = lambda t: jnp.asarray(t.astype(np.float32)).astype(BF16)
    return as_bf16(ctab), as_bf16(gtab), as_bf16(f2)


def _fnet(u, nb, length):
    n2 = F_GROUP_DIM
    n1 = length // n2
    ctab, gtab, f2 = _fnet_tables(length)
    rows = max(n1 * (n2 + FNET_PAD), n2 * (n1 + FNET_PAD))
    scr = pltpu.VMEM((rows, n2), F32)
    return pl.pallas_call(
        functools.partial(_fnet_kernel, n1),
        out_shape=jax.ShapeDtypeStruct((nb * length, F_GROUPS * n2), BF16),
        grid=(nb, F_GROUPS),
        in_specs=[pl.BlockSpec((length, n2), lambda b, g: (b, g)),
                  pl.BlockSpec((n2, 2 * n2), lambda b, g: (0, 0)),
                  pl.BlockSpec((n2, 2 * n1, 2 * n1), lambda b, g: (0, 0, 0),
                               pipeline_mode=pl.Buffered(1)),
                  pl.BlockSpec((n2, 2 * n2), lambda b, g: (0, 0))],
        out_specs=pl.BlockSpec((length, n2), lambda b, g: (b, g)),
        scratch_shapes=[scr, scr, scr, scr],
        compiler_params=_cparams(("parallel", "parallel")),
        name="fnet",
    )(u, ctab, gtab, f2)


def _merge_kernel(x_ref, mix_ref, y_ref, zs_ref, sgf_ref, sgs_ref, wfa_ref, wsb_ref, wo_ref,
                  ng_ref, gate_ref, n2g_ref, sh_ref, sc_ref, lat_ref, h2_ref):
    branch_f = _dot(mix_ref[...], wfa_ref[...])
    parts = []
    for g in range(SSD_GROUPS):
        sl = slice(g * SSD_GW, (g + 1) * SSD_GW)
        v = y_ref[:, sl].astype(F32) * zs_ref[:, sl].astype(F32)
        v = v * lax.rsqrt(jnp.mean(v * v, axis=-1, keepdims=True) + EPS)
        parts.append((v * ng_ref[:, sl]).astype(BF16))
    branch_s = _dot(jnp.concatenate(parts, axis=1), wsb_ref[...])
    merged = sgf_ref[...].astype(F32) * branch_f + sgs_ref[...].astype(F32) * branch_s
    lat = x_ref[...] + gate_ref[0] * _dot(merged.astype(BF16), wo_ref[...])
    lat_ref[...] = lat
    h2_ref[...] = _rms_modulate(lat, n2g_ref[...], sh_ref[0], sc_ref[0]).astype(h2_ref.dtype)


def _merge(x2, mixed, y, zs, sgf, sgs, wfa, wsb, wo, ng, gate, n2g, shift, scale, rows_per_batch):
    rows = x2.shape[0]
    tm = min(IN_TM, rows_per_batch)
    tpb = rows_per_batch // tm
    row_spec = lambda w: pl.BlockSpec((tm, w), lambda i: (i, 0))
    full = lambda a: pl.BlockSpec(a.shape, lambda i: (0,) * a.ndim, pipeline_mode=pl.Buffered(1))
    bspec = pl.BlockSpec((1, 1, D_MODEL), lambda i: (i // tpb, 0, 0))
    return pl.pallas_call(
        _merge_kernel,
        out_shape=[jax.ShapeDtypeStruct((rows, D_MODEL), F32),
                   jax.ShapeDtypeStruct((rows, D_MODEL), BF16)],
        grid=(rows // tm,),
        in_specs=[row_spec(D_MODEL), row_spec(D_MODEL), row_spec(SSD_INNER), row_spec(SSD_INNER),
                  row_spec(D_MODEL), row_spec(D_MODEL), full(wfa), full(wsb), full(wo),
                  pl.BlockSpec((1, SSD_INNER), lambda i: (0, 0)), bspec,
                  pl.BlockSpec((1, D_MODEL), lambda i: (0, 0)), bspec, bspec],
        out_specs=[row_spec(D_MODEL), row_spec(D_MODEL)],
        compiler_params=_cparams(("parallel",)),
        name="merge",
    )(x2, mixed, y, zs, sgf, sgs, wfa, wsb, wo, ng, gate, n2g, shift, scale)


def _ffn_kernel(tiles_per_img, nj, hm_ref, hp_ref, hn_ref, *refs):
    n_w = 7 * FFN_HALVES
    wrefs = [refs[7 * r:7 * r + 7] for r in range(FFN_HALVES)]
    lat_ref, gate_ref, fg_ref, o_ref, hext_s, ug_s, uv_s, acc_s = refs[n_w:]
    i = pl.program_id(0)
    m = pl.program_id(1)
    tm = hm_ref.shape[0]
    ext = tm + 2 * GRID_W
    nrow = tm // GRID_W
    tiles = GRID_W // F32_SUBLANES
    shape4 = (nrow + 2, tiles, F32_SUBLANES, FFN_CW)

    @pl.when(jnp.logical_and(i == 0, m == 0))
    def _():
        ug_s[1] = jnp.zeros((ext, FFN_CW), F32)
        uv_s[1] = jnp.zeros((ext, FFN_CW), F32)

    @pl.when(m == 0)
    def _():
        first = (i % tiles_per_img) == 0
        last = (i % tiles_per_img) == tiles_per_img - 1
        hp = hp_ref[...]
        hn = hn_ref[...]
        hext_s[0:GRID_W, :] = jnp.where(first, jnp.zeros_like(hp), hp)
        hext_s[GRID_W:GRID_W + tm, :] = hm_ref[...]
        hext_s[GRID_W + tm:ext, :] = jnp.where(last, jnp.zeros_like(hn), hn)
        acc_s[...] = jnp.zeros_like(acc_s)

    sub = lax.broadcasted_iota(jnp.int32, shape4, 2)
    zero_tile = jnp.zeros((nrow + 2, 1, F32_SUBLANES, FFN_CW), F32)

    def conv(u, cw_ref, cb_ref):
        u = u.reshape(shape4)
        rl = pltpu.roll(u, 1, 2)
        ul = jnp.where(sub == 0, jnp.concatenate([zero_tile, rl[:, :-1]], axis=1), rl)
        rr = pltpu.roll(u, F32_SUBLANES - 1, 2)
        ur = jnp.where(sub == F32_SUBLANES - 1, jnp.concatenate([rr[:, 1:], zero_tile], axis=1), rr)
        cw = cw_ref[...]
        out = cb_ref[...]
        for ky in range(3):
            r = slice(ky, ky + nrow)
            out = out + ul[r] * cw[3 * ky:3 * ky + 1] + u[r] * cw[3 * ky + 1:3 * ky + 2] \
                + ur[r] * cw[3 * ky + 2:3 * ky + 3]
        return out.reshape(tm, FFN_CW)

    he = hext_s[...]
    for r in range(FFN_HALVES):
        wg_ref, wv_ref, cwg_ref, cwv_ref, cbg_ref, cbv_ref, wd_ref = wrefs[r]
        wr, rd = r % 2, (r + 1) % 2
        ug_s[wr] = _dot(he, wg_ref[...])
        uv_s[wr] = _dot(he, wv_ref[...])
        gate = conv(ug_s[rd], cwg_ref, cbg_ref)
        val = conv(uv_s[rd], cwv_ref, cbv_ref)
        if r == 0:
            val = val * jnp.where(m == 0, 0.0, 1.0).astype(F32)
        act = (gate * _sigmoid(gate) * val).astype(BF16)
        acc_s[...] += _dot(act, wd_ref[...])

    @pl.when(m == pl.num_programs(1) - 1)
    def _():
        lat = lat_ref[...] + gate_ref[0] * acc_s[...]
        ms = jnp.mean(lat * lat, axis=-1, keepdims=True)
        o_ref[...] = (lat * lax.rsqrt(ms + EPS)) * fg_ref[...]


def _conv_ffn(h2, wup, cw, cb, wd, lat1, gate2, fg, rows_per_img):
    rows = h2.shape[0]
    tm = min(FFN_TM, rows_per_img)
    tiles_per_img = rows_per_img // tm
    nj = D_FF // FFN_CW
    steps = (nj + 1) // FFN_HALVES
    hb = tm // GRID_W
    nhb = rows // GRID_W
    ext = tm + 2 * GRID_W
    w_specs, w_args = [], []
    for r in range(FFN_HALVES):
        up = lambda m, r=r: jnp.minimum(FFN_HALVES * m + r, nj - 1)
        cv = lambda m, r=r: jnp.maximum(FFN_HALVES * m + r - 1, 0)
        w_specs += [pl.BlockSpec((D_MODEL, FFN_CW), lambda i, m, f=up: (0, f(m))),
                    pl.BlockSpec((D_MODEL, FFN_CW), lambda i, m, f=up: (0, nj + f(m))),
                    pl.BlockSpec((9, FFN_CW), lambda i, m, f=cv: (0, f(m))),
                    pl.BlockSpec((9, FFN_CW), lambda i, m, f=cv: (0, nj + f(m))),
                    pl.BlockSpec((1, FFN_CW), lambda i, m, f=cv: (0, f(m))),
                    pl.BlockSpec((1, FFN_CW), lambda i, m, f=cv: (0, nj + f(m))),
                    pl.BlockSpec((FFN_CW, D_MODEL), lambda i, m, f=cv: (f(m), 0))]
        w_args += [wup, wup, cw, cw, cb, cb, wd]
    return pl.pallas_call(
        functools.partial(_ffn_kernel, tiles_per_img, nj),
        out_shape=jax.ShapeDtypeStruct((rows, D_MODEL), F32),
        grid=(rows // tm, steps),
        in_specs=[pl.BlockSpec((tm, D_MODEL), lambda i, m: (i, 0)),
                  pl.BlockSpec((GRID_W, D_MODEL), lambda i, m: (jnp.maximum(i * hb - 1, 0), 0)),
                  pl.BlockSpec((GRID_W, D_MODEL), lambda i, m: (jnp.minimum((i + 1) * hb, nhb - 1), 0))]
                 + w_specs
                 + [pl.BlockSpec((tm, D_MODEL), lambda i, m: (i, 0)),
                    pl.BlockSpec((1, 1, D_MODEL), lambda i, m: (i // tiles_per_img, 0, 0)),
                    pl.BlockSpec((1, D_MODEL), lambda i, m: (0, 0))],
        out_specs=pl.BlockSpec((tm, D_MODEL), lambda i, m: (i, 0)),
        scratch_shapes=[pltpu.VMEM((ext, D_MODEL), BF16),
                        pltpu.VMEM((2, ext, FFN_CW), F32),
                        pltpu.VMEM((2, ext, FFN_CW), F32),
                        pltpu.VMEM((tm, D_MODEL), F32)],
        compiler_params=_cparams(("arbitrary", "arbitrary")),
        name="conv_ffn",
    )(h2, h2, h2, *w_args, lat1, gate2, fg)


def _dt_perm():
    idx = np.arange(2 * SSD_HEADS).reshape(2, SSD_GROUPS, SSD_HPG)
    return idx.transpose(1, 0, 2).reshape(-1)


def _expansion_matrices():
    mats = []
    for d in range(2):
        e = np.zeros((LANES, SSD_INNER), np.float32)
        for g in range(SSD_GROUPS):
            for h in range(SSD_HPG):
                c0 = g * SSD_GW + h * SSD_HEAD_DIM
                e[(2 * g + d) * SSD_HPG + h, c0:c0 + SSD_HEAD_DIM] = 1.0
        e[2 * SSD_HEADS:] = e[:2 * SSD_HEADS]
        mats.append(jnp.asarray(e).astype(BF16))
    return mats


def kernel(x, c, ctx, c_ctx, w_mod, b_mod, norm1_g, w_in, conv_ssd_w, conv_ssd_b, dt_bias, a_log,
           d_skip, ssd_norm_g, w_fa, w_sb, w_o, norm2_g, w_up, conv_ffn_w, conv_ffn_b, w_down, final_g):
    assert w_mod.shape[0] == 1, "single-layer block"
    nb, length, _ = x.shape
    clen = ctx.shape[1]
    assert length % FFN_TM == 0 and clen % SSD_CHUNK == 0

    cond = jnp.zeros((8, D_MODEL), F32).at[:nb].set(c).at[nb].set(c_ctx)
    mods = _adaln(cond, w_mod[0], b_mod[0][None, :])
    m_lat = [mods[:nb, k * D_MODEL:(k + 1) * D_MODEL][:, None, :] for k in range(N_MOD)]
    m_ctx = [mods[nb:nb + 1, k * D_MODEL:(k + 1) * D_MODEL][:, None, :] for k in range(2)]

    w = w_in[0]
    o_f, o_xbc, o_z = D_MODEL, D_MODEL + XBC_WIDTH, D_MODEL + XBC_WIDTH + SSD_INNER
    o_dt = o_z + 2 * SSD_HEADS
    perm = _dt_perm()
    w_dt = jnp.pad(w[:, o_z:o_dt][:, perm], ((0, 0), (0, LANES - 2 * SSD_HEADS)))
    w_lat = jnp.concatenate([w[:, :o_f], w[:, o_f:o_xbc], w[:, o_xbc:o_z], w[:, o_dt:o_dt + D_MODEL],
                             w[:, o_dt + D_MODEL:], w_dt], axis=1).astype(BF16)
    w_ctx = jnp.concatenate([w[:, o_f:o_xbc], w_dt], axis=1).astype(BF16)
    dtb = jnp.pad(dt_bias[0].reshape(-1)[perm], (0, LANES - 2 * SSD_HEADS))[None, :]
    g1 = norm1_g[0][None, :]

    x2 = x.reshape(nb * length, D_MODEL)
    cw, cb = conv_ssd_w[0], conv_ssd_b[0][None, :]
    segs_lat = (("id", D_MODEL), ("conv_silu", XBC_WIDTH), ("silu", SSD_INNER), ("sigmoid", D_MODEL),
                ("sigmoid", D_MODEL), ("softplus", LANES))
    u_f, xbc_l, zs, sgf, sgs, dt_l = _inproj(
        x2, m_lat[0], m_lat[1], g1, w_lat, dtb, cw, cb, segs_lat, (BF16, BF16, BF16, BF16, BF16, F32),
        length)
    segs_ctx = (("conv_silu", XBC_WIDTH), ("softplus", LANES))
    xbc_c, dt_c = _inproj(ctx.reshape(nb * clen, D_MODEL), m_ctx[0], m_ctx[1], g1, w_ctx, dtb, cw, cb,
                          segs_ctx, (BF16, F32), clen)

    alog = jnp.pad(a_log[0].reshape(-1)[perm], (0, LANES - 2 * SSD_HEADS))[None, :]
    dsk = jnp.repeat(d_skip[0], SSD_HEAD_DIM)[None, :]
    exp_f, exp_b = _expansion_matrices()
    zeros = jnp.zeros((nb, SSD_GROUPS, SSD_STATE, SSD_GW), F32)
    _, s_f, s_b = _ssd_scan(xbc_c, dt_c, alog, dsk, exp_f, exp_b, zeros, zeros,
                            nb, clen)
    y, _, _ = _ssd_scan(xbc_l, dt_l, alog, dsk, exp_f, exp_b, s_f, s_b,
                        nb, length)

    mixed = _fnet(u_f, nb, length)

    lat1, h2 = _merge(x2, mixed, y, zs, sgf, sgs, w_fa[0].astype(BF16), w_sb[0].astype(BF16),
                      w_o[0].astype(BF16), ssd_norm_g[0][None, :], m_lat[2], norm2_g[0][None, :],
                      m_lat[3], m_lat[4], length)

    out = _conv_ffn(h2, w_up[0].astype(BF16), conv_ffn_w[0].reshape(9, 2 * D_FF), conv_ffn_b[0][None, :],
                    w_down[0].astype(BF16), lat1, m_lat[5], final_g[None, :], length)
    return out.reshape(nb, length, D_MODEL)
```

```python
import functools

import numpy as np
import jax
import jax.numpy as jnp
from jax import lax
from jax.experimental import pallas as pl
from jax.experimental.pallas import tpu as pltpu

F32 = jnp.float32
BF16 = jnp.bfloat16

D_MODEL = 1024
N_MOD = 6
EPS = 1e-6
F_GROUPS = 8
F_GROUP_DIM = 128
SSD_HEADS = 32
SSD_HEAD_DIM = 64
SSD_INNER = SSD_HEADS * SSD_HEAD_DIM
SSD_GROUPS = 4
SSD_HPG = SSD_HEADS // SSD_GROUPS
SSD_STATE = 128
SSD_BC = SSD_GROUPS * SSD_STATE
SSD_CHUNK = 128
SSD_GW = SSD_HPG * SSD_HEAD_DIM
XBC_WIDTH = SSD_INNER + 2 * SSD_BC
D_FF = 2816
GRID_W = 64
LOG2E = 1.4426950408889634

LANES = 128
F32_SUBLANES = 8
BF16_SUBLANES = 16
VMEM_LIMIT = 56 * 1024 * 1024

SSD_BWD_CPS = 8
SSD_FWD_CPS = 4
IN_TM = 512
IN_CHUNK = 512
FFN_TM = 1024
FFN_CW = 256
FFN_HALVES = 2
FNET_SLABS = 8
FNET_PAD = 8
FNET_UNROLL = 32
FNET_UNROLL0 = 4


def _cparams(sem):
    return pltpu.CompilerParams(dimension_semantics=sem, vmem_limit_bytes=VMEM_LIMIT)


def _sigmoid(v):
    return 1.0 / (1.0 + jnp.exp(-v))


def _dot(a, b):
    return jnp.dot(a, b, preferred_element_type=F32)


def _split2(v):
    hi = v.astype(BF16)
    lo = (v - hi.astype(F32)).astype(BF16)
    return hi, lo


def _split3(v):
    hi = v.astype(BF16)
    r = v - hi.astype(F32)
    mid = r.astype(BF16)
    lo = (r - mid.astype(F32)).astype(BF16)
    return hi, mid, lo


def _mod_kernel(c_ref, w_ref, b_ref, o_ref):
    c = c_ref[...]
    s = c * _sigmoid(c)
    w = w_ref[...]
    s_hi, s_lo = _split2(s)
    w_hi, w_lo = _split2(w)
    o_ref[...] = _dot(s_hi, w_hi) + _dot(s_lo, w_hi) + _dot(s_hi, w_lo) + b_ref[...]


def _adaln(cond8, w_mod, b_mod):
    n = w_mod.shape[1]
    tn = 1024
    return pl.pallas_call(
        _mod_kernel,
        out_shape=jax.ShapeDtypeStruct((8, n), F32),
        grid=(n // tn,),
        in_specs=[pl.BlockSpec((8, D_MODEL), lambda j: (0, 0)),
                  pl.BlockSpec((D_MODEL, tn), lambda j: (0, j)),
                  pl.BlockSpec((1, tn), lambda j: (0, j))],
        out_specs=pl.BlockSpec((8, tn), lambda j: (0, j)),
        compiler_params=_cparams(("parallel",)),
        name="adaln_mod",
    )(cond8, w_mod, b_mod)


def _rms_modulate(x, g, shift, scale):
    ms = jnp.mean(x * x, axis=-1, keepdims=True)
    h = (x * lax.rsqrt(ms + EPS)) * g
    return h * (1.0 + scale) + shift


def _inproj_kernel(segs, tiles_per_seq, x_ref, xp_ref, xn_ref, sh_ref, sc_ref, g_ref, w_ref, dtb_ref,
                   cw_ref, cb_ref, *o_refs):
    i = pl.program_id(0)
    first = (i % tiles_per_seq) == 0
    last = (i % tiles_per_seq) == tiles_per_seq - 1
    tm = x_ref.shape[0]
    nt = tm // F32_SUBLANES
    g, shift, scale = g_ref[...], sh_ref[0], sc_ref[0]
    hb = _rms_modulate(x_ref[...], g, shift, scale).astype(BF16)
    halo = jnp.concatenate([xp_ref[...], xn_ref[...]], axis=0)
    hh = _rms_modulate(halo, g, shift, scale).astype(BF16)
    work = []
    col = 0
    for (kind, width), o_ref in zip(segs, o_refs):
        work += [(kind, o_ref, col + c0, c0, min(IN_CHUNK, width - c0)) for c0 in range(0, width, IN_CHUNK)]
        col += width
    hall = jnp.concatenate([hb, hh], axis=0)
    for kind, o_ref, wcol, c0, cw in work:
        wc = w_ref[:, wcol:wcol + cw]
        if kind == "conv_silu":
            r = _dot(hall, wc)
            r, rh = r[:tm], r[tm:]
        else:
            r = _dot(hb, wc)
        if kind == "silu":
            r = r * _sigmoid(r)
        elif kind == "sigmoid":
            r = _sigmoid(r)
        elif kind == "softplus":
            r = r + dtb_ref[...]
            r = jnp.maximum(r, 0.0) + jnp.log(1.0 + jnp.exp(-jnp.abs(r)))
        elif kind == "conv_silu":
            prev = jnp.where(first, 0.0, rh[F32_SUBLANES - 1:F32_SUBLANES])
            nxt = jnp.where(last, 0.0, rh[F32_SUBLANES:F32_SUBLANES + 1])
            prev = jnp.broadcast_to(prev[None], (1, F32_SUBLANES, cw))
            nxt = jnp.broadcast_to(nxt[None], (1, F32_SUBLANES, cw))
            sub = lax.broadcasted_iota(jnp.int32, (nt, F32_SUBLANES, cw), 1)
            x = r.reshape(nt, F32_SUBLANES, cw)
            rl = pltpu.roll(x, 1, 1)
            xl = jnp.where(sub == 0, jnp.concatenate([prev, rl[:-1]], axis=0), rl)
            rr = pltpu.roll(x, F32_SUBLANES - 1, 1)
            xr = jnp.where(sub == F32_SUBLANES - 1, jnp.concatenate([rr[1:], nxt], axis=0), rr)
            k = cw_ref[:, c0:c0 + cw]
            y = xl * k[0:1] + x * k[1:2] + xr * k[2:3] + cb_ref[:, c0:c0 + cw]
            r = (y * _sigmoid(y)).reshape(tm, cw)
        o_ref[:, c0:c0 + cw] = r.astype(o_ref.dtype)


def _inproj(x2, shift, scale, g, w, dtb, cw, cb, segs, out_dtypes, rows_per_batch):
    rows = x2.shape[0]
    tm = min(IN_TM, rows_per_batch)
    tiles_per_batch = rows_per_batch // tm
    nb = shift.shape[0]
    bidx = (lambda i: (i // tiles_per_batch, 0, 0)) if nb > 1 else (lambda i: (0, 0, 0))
    wtot = w.shape[1]
    hb = tm // F32_SUBLANES
    nhb = rows // F32_SUBLANES
    return pl.pallas_call(
        functools.partial(_inproj_kernel, segs, tiles_per_batch),
        out_shape=[jax.ShapeDtypeStruct((rows, wd), dt) for (_, wd), dt in zip(segs, out_dtypes)],
        grid=(rows // tm,),
        in_specs=[pl.BlockSpec((tm, D_MODEL), lambda i: (i, 0)),
                  pl.BlockSpec((F32_SUBLANES, D_MODEL), lambda i: (jnp.maximum(i * hb - 1, 0), 0)),
                  pl.BlockSpec((F32_SUBLANES, D_MODEL), lambda i: (jnp.minimum((i + 1) * hb, nhb - 1), 0)),
                  pl.BlockSpec((1, 1, D_MODEL), bidx),
                  pl.BlockSpec((1, 1, D_MODEL), bidx),
                  pl.BlockSpec((1, D_MODEL), lambda i: (0, 0)),
                  pl.BlockSpec((D_MODEL, wtot), lambda i: (0, 0), pipeline_mode=pl.Buffered(1)),
                  pl.BlockSpec((1, LANES), lambda i: (0, 0)),
                  pl.BlockSpec((3, XBC_WIDTH), lambda i: (0, 0)),
                  pl.BlockSpec((1, XBC_WIDTH), lambda i: (0, 0))],
        out_specs=[pl.BlockSpec((tm, wd), lambda i: (i, 0)) for (_, wd) in segs],
        compiler_params=_cparams(("parallel",)),
        name="inproj",
    )(x2, x2, x2, shift, scale, g, w, dtb, cw, cb)


def _expand(v, e_ref):
    lane = lax.broadcasted_iota(jnp.int32, v.shape, 1)
    v = jnp.where(lane < 2 * SSD_HEADS, v, 0.0)
    hi = v.astype(BF16).astype(F32)
    stacked = hi + pltpu.roll(v - hi, 2 * SSD_HEADS, 1)
    return _dot(stacked.astype(BF16), e_ref[...])


def _chunk_decays(dt, alog):
    q = SSD_CHUNK
    row = lax.broadcasted_iota(jnp.int32, (q, LANES), 0)
    lane = lax.broadcasted_iota(jnp.int32, (q, LANES), 1)
    dtv = jnp.where(lane < 2 * SSD_HEADS, dt, 0.0)
    la = dtv * (-jnp.exp(alog) * LOG2E)
    tri = jnp.where(row >= lane, 1.0, 0.0).astype(BF16)
    la_hi, la_mid, la_lo = _split3(la)
    cum = _dot(tri, la_hi) + _dot(tri, la_mid) + _dot(tri, la_lo)
    excl = cum - la
    tot = cum[q - 1:q, :]
    is_f = (lane & SSD_HPG) == 0
    e_in = jnp.exp2(jnp.where(is_f, cum, tot - excl))
    w_out = jnp.exp2(jnp.where(is_f, tot - cum, excl)) * dtv
    return dtv, cum, excl, e_in, w_out


def _local_state(bm, xw):
    return lax.dot_general(bm, xw, (((0,), (0,)), ((), ())), preferred_element_type=F32)


def _ssd_bwd_kernel(steps, cps, x_ref, b_ref, dt_ref, alog_ref, eb_ref, sb0_ref, enter_ref, sbo_ref,
                    sb_s):
    t = pl.program_id(1)
    q = SSD_CHUNK

    @pl.when(t == 0)
    def _():
        sb_s[...] = sb0_ref[0]

    tabs = []
    for k in range(cps):
        _, _, _, e_in, w_out = _chunk_decays(dt_ref[k * q:(k + 1) * q, :], alog_ref[...])
        both = _expand(jnp.concatenate([w_out, e_in[0:BF16_SUBLANES]], axis=0), eb_ref)
        tabs.append((both[:q], both[q:q + 1]))
    for k in reversed(range(cps)):
        rows = slice(k * q, (k + 1) * q)
        w_b, dec = tabs[k]
        enter_ref[k] = sb_s[...].astype(BF16)
        xw = (x_ref[rows, :].astype(F32) * w_b).astype(BF16)
        for g in range(SSD_GROUPS):
            sl = slice(g * SSD_GW, (g + 1) * SSD_GW)
            loc = _local_state(b_ref[rows, g * SSD_STATE:(g + 1) * SSD_STATE], xw[:, sl])
            sb_s[g] = dec[:, sl] * sb_s[g] + loc

    @pl.when(t == steps - 1)
    def _():
        sbo_ref[0] = sb_s[...]


def _ssd_fwd_kernel(steps, cps, x_ref, b_ref, c_ref, dt_ref, alog_ref, dsk_ref, ef_ref, eb_ref, sf0_ref,
                    enter_ref, y_ref, sfo_ref, sf_s):
    t = pl.program_id(1)
    q = SSD_CHUNK

    @pl.when(t == 0)
    def _():
        sf_s[...] = sf0_ref[0]

    row = lax.broadcasted_iota(jnp.int32, (q, LANES), 0)
    lane = lax.broadcasted_iota(jnp.int32, (q, LANES), 1)
    lower = row >= lane
    diag = row == lane
    for k in range(cps):
        rows = slice(k * q, (k + 1) * q)
        dtv, cum, excl, e_in, w_out = _chunk_decays(dt_ref[rows, :], alog_ref[...])
        cum_t = cum.T
        excl_t = excl.T
        dt_t = dtv.T
        both = _expand(jnp.concatenate([e_in, w_out], axis=0), ef_ref)
        ef_exp, wf_exp = both[:q], both[q:]
        eb_exp = _expand(e_in, eb_ref)
        for g in range(SSD_GROUPS):
            sl = slice(g * SSD_GW, (g + 1) * SSD_GW)
            bm = b_ref[rows, g * SSD_STATE:(g + 1) * SSD_STATE]
            cm = c_ref[rows, g * SSD_STATE:(g + 1) * SSD_STATE]
            xf = x_ref[rows, sl].astype(F32)
            scores = lax.dot_general(cm, bm, (((1,), (1,)), ((), ())), preferred_element_type=F32)
            ys = []
            for j in range(SSD_HPG // 2):
                ms = []
                for h in (2 * j, 2 * j + 1):
                    lf = 2 * SSD_HPG * g + h
                    lb = lf + SSD_HPG
                    expo = jnp.where(lower,
                                     cum[:, lf:lf + 1] - cum_t[lf:lf + 1, :],
                                     excl_t[lb:lb + 1, :] - excl[:, lb:lb + 1])
                    dtf = dt_t[lf:lf + 1, :]
                    dtb = dt_t[lb:lb + 1, :]
                    dsel = jnp.where(diag, dtf + dtb, jnp.where(lower, dtf, dtb))
                    ms.append((scores * jnp.exp2(expo) * dsel).astype(BF16))
                xp = xf[:, j * LANES:(j + 1) * LANES]
                top = jnp.where(lane < SSD_HEAD_DIM, xp, 0.0).astype(BF16)
                bot = jnp.where(lane >= SSD_HEAD_DIM, xp, 0.0).astype(BF16)
                ys.append(_dot(jnp.concatenate(ms, axis=1), jnp.concatenate([top, bot], axis=0)))
            y = jnp.concatenate(ys, axis=1)
            y = y + ef_exp[:, sl] * _dot(cm, sf_s[g].astype(BF16))
            y = y + eb_exp[:, sl] * _dot(cm, enter_ref[k, g])
            y = y + dsk_ref[:, sl] * xf
            y_ref[rows, sl] = y.astype(y_ref.dtype)
            xw = (xf * wf_exp[:, sl]).astype(BF16)
            sf_s[g] = ef_exp[q - 1:q, sl] * sf_s[g] + _local_state(bm, xw)

    @pl.when(t == steps - 1)
    def _():
        sfo_ref[0] = sf_s[...]


def _ssd_scan(xa, dt, alog, dsk, exp_f, exp_b, sf0, sb0, nb, length):
    nc = length // SSD_CHUNK
    rows = xa.shape[0]
    sdims = (SSD_GROUPS, SSD_STATE, SSD_GW)
    state_shape = jax.ShapeDtypeStruct((nb,) + sdims, F32)
    state_spec = pl.BlockSpec((1,) + sdims, lambda b, t: (b, 0, 0, 0))
    enter_shape = jax.ShapeDtypeStruct((nb * nc,) + sdims, BF16)
    const = lambda a: pl.BlockSpec(a.shape, lambda b, t: (0,) * a.ndim)
    bcol = SSD_INNER // SSD_BC
    scratch = [pltpu.VMEM(sdims, F32)]

    cps = SSD_BWD_CPS if nc % SSD_BWD_CPS == 0 else 1
    steps = nc // cps
    q = SSD_CHUNK * cps
    rev = lambda b, t: b * steps + steps - 1 - t
    enter, s_b = pl.pallas_call(
        functools.partial(_ssd_bwd_kernel, steps, cps),
        out_shape=[enter_shape, state_shape],
        grid=(nb, steps),
        in_specs=[pl.BlockSpec((q, SSD_INNER), lambda b, t: (rev(b, t), 0)),
                  pl.BlockSpec((q, SSD_BC), lambda b, t: (rev(b, t), bcol)),
                  pl.BlockSpec((q, LANES), lambda b, t: (rev(b, t), 0)),
                  const(alog), const(exp_b), state_spec],
        out_specs=[pl.BlockSpec((cps,) + sdims, lambda b, t: (rev(b, t), 0, 0, 0)), state_spec],
        scratch_shapes=scratch,
        compiler_params=_cparams(("parallel", "arbitrary")),
        name="ssd_bwd_states",
    )(xa, xa, dt, alog, exp_b, sb0)

    cps = SSD_FWD_CPS if nc % SSD_FWD_CPS == 0 else 1
    steps = nc // cps
    q = SSD_CHUNK * cps
    fwd = lambda b, t: b * steps + t
    y, s_f = pl.pallas_call(
        functools.partial(_ssd_fwd_kernel, steps, cps),
        out_shape=[jax.ShapeDtypeStruct((rows, SSD_INNER), BF16), state_shape],
        grid=(nb, steps),
        in_specs=[pl.BlockSpec((q, SSD_INNER), lambda b, t: (fwd(b, t), 0)),
                  pl.BlockSpec((q, SSD_BC), lambda b, t: (fwd(b, t), bcol)),
                  pl.BlockSpec((q, SSD_BC), lambda b, t: (fwd(b, t), bcol + 1)),
                  pl.BlockSpec((q, LANES), lambda b, t: (fwd(b, t), 0)),
                  const(alog), const(dsk), const(exp_f), const(exp_b), state_spec,
                  pl.BlockSpec((cps,) + sdims, lambda b, t: (fwd(b, t), 0, 0, 0))],
        out_specs=[pl.BlockSpec((q, SSD_INNER), lambda b, t: (fwd(b, t), 0)), state_spec],
        scratch_shapes=scratch,
        compiler_params=_cparams(("parallel", "arbitrary")),
        name="ssd_scan",
    )(xa, xa, xa, dt, alog, dsk, exp_f, exp_b, sf0, enter)
    return y, s_f, s_b


def _fnet_kernel(n1, u_ref, ctab_ref, g_ref, f2_ref, o_ref, zr_s, zi_s, ar_s, ai_s):
    n2 = F_GROUP_DIM
    pz = n2 + FNET_PAD
    po = n1 + FNET_PAD
    slabs = min(FNET_SLABS, n1)
    out_s = zr_s

    def stage0(i, carry):
        pq = _dot(u_ref[pl.ds(pl.multiple_of(i * slabs * n2, n2), slabs * n2), :], ctab_ref[...])
        for j in range(slabs):
            dst = pl.multiple_of((i * slabs + j) * pz, 8)
            zr_s[pl.ds(dst, n2), :] = pq[j * n2:(j + 1) * n2, :n2]
            zi_s[pl.ds(dst, n2), :] = pq[j * n2:(j + 1) * n2, n2:]
        return carry

    lax.fori_loop(0, n1 // slabs, stage0, 0, unroll=min(FNET_UNROLL0, n1 // slabs))

    def stage1(b, carry):
        z = jnp.concatenate([zr_s[pl.ds(b, n1, stride=pz), :],
                             zi_s[pl.ds(b, n1, stride=pz), :]], axis=0).astype(BF16)
        a = _dot(g_ref[b], z)
        ar_s[pl.ds(b, n1, stride=pz), :] = a[:n1]
        ai_s[pl.ds(b, n1, stride=pz), :] = a[n1:]
        return carry

    lax.fori_loop(0, n2, stage1, 0, unroll=min(FNET_UNROLL, n2))

    def stage2(k1, carry):
        r0 = pl.multiple_of(k1 * pz, 8)
        a = jnp.concatenate([ar_s[pl.ds(r0, n2), :], ai_s[pl.ds(r0, n2), :]], axis=0).astype(BF16)
        out_s[pl.ds(k1, n2, stride=po), :] = _dot(f2_ref[...], a)
        return carry

    lax.fori_loop(0, n1, stage2, 0, unroll=min(FNET_UNROLL, n1))

    def stage3(k2, carry):
        src = pl.multiple_of(k2 * po, 8)
        dst = pl.multiple_of(k2 * n1, n1)
        o_ref[pl.ds(dst, n1), :] = out_s[pl.ds(src, n1), :].astype(o_ref.dtype)
        return carry

    lax.fori_loop(0, n2, stage3, 0, unroll=min(FNET_UNROLL, n2))


def _fnet_tables(length):
    n2 = F_GROUP_DIM
    n1 = length // n2
    j = np.arange(n2, dtype=np.int64)
    ang_c = 2.0 * np.pi * ((j[:, None] * j[None, :]) % n2) / n2
    ctab = np.concatenate([np.cos(ang_c), -np.sin(ang_c)], axis=1)
    k1 = np.arange(n1, dtype=np.int64)[None, :, None]
    a = np.arange(n1, dtype=np.int64)[None, None, :]
    b = np.arange(n2, dtype=np.int64)[:, None, None]
    ang = 2.0 * np.pi * ((k1 * (n2 * a + b)) % length) / length
    cs, sn = np.cos(ang), np.sin(ang)
    gtab = np.concatenate([np.concatenate([cs, sn], axis=2),
                           np.concatenate([-sn, cs], axis=2)], axis=1)
    scale = 1.0 / np.sqrt(float(length * n2))
    f2 = np.concatenate([np.cos(ang_c), np.sin(ang_c)], axis=1) * scale
    as_bf16 = lambda t: jnp.asarray(t.astype(np.float32)).astype(BF16)
    return as_bf16(ctab), as_bf16(gtab), as_bf16(f2)


def _fnet(u, nb, length):
    n2 = F_GROUP_DIM
    n1 = length // n2
    ctab, gtab, f2 = _fnet_tables(length)
    rows = max(n1 * (n2 + FNET_PAD), n2 * (n1 + FNET_PAD))
    scr = pltpu.VMEM((rows, n2), F32)
    return pl.pallas_call(
        functools.partial(_fnet_kernel, n1),
        out_shape=jax.ShapeDtypeStruct((nb * length, F_GROUPS * n2), BF16),
        grid=(nb, F_GROUPS),
        in_specs=[pl.BlockSpec((length, n2), lambda b, g: (b, g)),
                  pl.BlockSpec((n2, 2 * n2), lambda b, g: (0, 0)),
                  pl.BlockSpec((n2, 2 * n1, 2 * n1), lambda b, g: (0, 0, 0),
                               pipeline_mode=pl.Buffered(1)),
                  pl.BlockSpec((n2, 2 * n2), lambda b, g: (0, 0))],
        out_specs=pl.BlockSpec((length, n2), lambda b, g: (b, g)),
        scratch_shapes=[scr, scr, scr, scr],
        compiler_params=_cparams(("parallel", "parallel")),
        name="fnet",
    )(u, ctab, gtab, f2)


def _merge_kernel(x_ref, mix_ref, y_ref, zs_ref, sgf_ref, sgs_ref, wfa_ref, wsb_ref, wo_ref,
                  ng_ref, gate_ref, n2g_ref, sh_ref, sc_ref, lat_ref, h2_ref):
    branch_f = _dot(mix_ref[...], wfa_ref[...])
    parts = []
    for g in range(SSD_GROUPS):
        sl = slice(g * SSD_GW, (g + 1) * SSD_GW)
        v = y_ref[:, sl].astype(F32) * zs_ref[:, sl].astype(F32)
        v = v * lax.rsqrt(jnp.mean(v * v, axis=-1, keepdims=True) + EPS)
        parts.append((v * ng_ref[:, sl]).astype(BF16))
    branch_s = _dot(jnp.concatenate(parts, axis=1), wsb_ref[...])
    merged = sgf_ref[...].astype(F32) * branch_f + sgs_ref[...].astype(F32) * branch_s
    lat = x_ref[...] + gate_ref[0] * _dot(merged.astype(BF16), wo_ref[...])
    lat_ref[...] = lat
    h2_ref[...] = _rms_modulate(lat, n2g_ref[...], sh_ref[0], sc_ref[0]).astype(h2_ref.dtype)


def _merge(x2, mixed, y, zs, sgf, sgs, wfa, wsb, wo, ng, gate, n2g, shift, scale, rows_per_batch):
    rows = x2.shape[0]
    tm = min(IN_TM, rows_per_batch)
    tpb = rows_per_batch // tm
    row_spec = lambda w: pl.BlockSpec((tm, w), lambda i: (i, 0))
    full = lambda a: pl.BlockSpec(a.shape, lambda i: (0,) * a.ndim, pipeline_mode=pl.Buffered(1))
    bspec = pl.BlockSpec((1, 1, D_MODEL), lambda i: (i // tpb, 0, 0))
    return pl.pallas_call(
        _merge_kernel,
        out_shape=[jax.ShapeDtypeStruct((rows, D_MODEL), F32),
                   jax.ShapeDtypeStruct((rows, D_MODEL), BF16)],
        grid=(rows // tm,),
        in_specs=[row_spec(D_MODEL), row_spec(D_MODEL), row_spec(SSD_INNER), row_spec(SSD_INNER),
                  row_spec(D_MODEL), row_spec(D_MODEL), full(wfa), full(wsb), full(wo),
                  pl.BlockSpec((1, SSD_INNER), lambda i: (0, 0)), bspec,
                  pl.BlockSpec((1, D_MODEL), lambda i: (0, 0)), bspec, bspec],
        out_specs=[row_spec(D_MODEL), row_spec(D_MODEL)],
        compiler_params=_cparams(("parallel",)),
        name="merge",
    )(x2, mixed, y, zs, sgf, sgs, wfa, wsb, wo, ng, gate, n2g, shift, scale)


def _ffn_kernel(tiles_per_img, nj, hm_ref, hp_ref, hn_ref, *refs):
    n_w = 7 * FFN_HALVES
    wrefs = [refs[7 * r:7 * r + 7] for r in range(FFN_HALVES)]
    lat_ref, gate_ref, fg_ref, o_ref, hext_s, ug_s, uv_s, acc_s = refs[n_w:]
    i = pl.program_id(0)
    m = pl.program_id(1)
    tm = hm_ref.shape[0]
    ext = tm + 2 * GRID_W
    nrow = tm // GRID_W
    tiles = GRID_W // F32_SUBLANES
    shape4 = (nrow + 2, tiles, F32_SUBLANES, FFN_CW)

    @pl.when(jnp.logical_and(i == 0, m == 0))
    def _():
        ug_s[1] = jnp.zeros((ext, FFN_CW), F32)
        uv_s[1] = jnp.zeros((ext, FFN_CW), F32)

    @pl.when(m == 0)
    def _():
        first = (i % tiles_per_img) == 0
        last = (i % tiles_per_img) == tiles_per_img - 1
        hp = hp_ref[...]
        hn = hn_ref[...]
        hext_s[0:GRID_W, :] = jnp.where(first, jnp.zeros_like(hp), hp)
        hext_s[GRID_W:GRID_W + tm, :] = hm_ref[...]
        hext_s[GRID_W + tm:ext, :] = jnp.where(last, jnp.zeros_like(hn), hn)
        acc_s[...] = jnp.zeros_like(acc_s)

    sub = lax.broadcasted_iota(jnp.int32, shape4, 2)
    zero_tile = jnp.zeros((nrow + 2, 1, F32_SUBLANES, FFN_CW), F32)

    def conv(u, cw_ref, cb_ref):
        u = u.reshape(shape4)
        rl = pltpu.roll(u, 1, 2)
        ul = jnp.where(sub == 0, jnp.concatenate([zero_tile, rl[:, :-1]], axis=1), rl)
        rr = pltpu.roll(u, F32_SUBLANES - 1, 2)
        ur = jnp.where(sub == F32_SUBLANES - 1, jnp.concatenate([rr[:, 1:], zero_tile], axis=1), rr)
        cw = cw_ref[...]
        out = cb_ref[...]
        for ky in range(3):
            r = slice(ky, ky + nrow)
            out = out + ul[r] * cw[3 * ky:3 * ky + 1] + u[r] * cw[3 * ky + 1:3 * ky + 2] \
                + ur[r] * cw[3 * ky + 2:3 * ky + 3]
        return out.reshape(tm, FFN_CW)

    he = hext_s[...]
    for r in range(FFN_HALVES):
        wg_ref, wv_ref, cwg_ref, cwv_ref, cbg_ref, cbv_ref, wd_ref = wrefs[r]
        wr, rd = r % 2, (r + 1) % 2
        ug_s[wr] = _dot(he, wg_ref[...])
        uv_s[wr] = _dot(he, wv_ref[...])
        gate = conv(ug_s[rd], cwg_ref, cbg_ref)
        val = conv(uv_s[rd], cwv_ref, cbv_ref)
        if r == 0:
            val = val * jnp.where(m == 0, 0.0, 1.0).astype(F32)
        act = (gate * _sigmoid(gate) * val).astype(BF16)
        acc_s[...] += _dot(act, wd_ref[...])

    @pl.when(m == pl.num_programs(1) - 1)
    def _():
        lat = lat_ref[...] + gate_ref[0] * acc_s[...]
        ms = jnp.mean(lat * lat, axis=-1, keepdims=True)
        o_ref[...] = (lat * lax.rsqrt(ms + EPS)) * fg_ref[...]


def _conv_ffn(h2, wup, cw, cb, wd, lat1, gate2, fg, rows_per_img):
    rows = h2.shape[0]
    tm = min(FFN_TM, rows_per_img)
    tiles_per_img = rows_per_img // tm
    nj = D_FF // FFN_CW
    steps = (nj + 1) // FFN_HALVES
    hb = tm // GRID_W
    nhb = rows // GRID_W
    ext = tm + 2 * GRID_W
    w_specs, w_args = [], []
    for r in range(FFN_HALVES):
        up = lambda m, r=r: jnp.minimum(FFN_HALVES * m + r, nj - 1)
        cv = lambda m, r=r: jnp.maximum(FFN_HALVES * m + r - 1, 0)
        w_specs += [pl.BlockSpec((D_MODEL, FFN_CW), lambda i, m, f=up: (0, f(m))),
                    pl.BlockSpec((D_MODEL, FFN_CW), lambda i, m, f=up: (0, nj + f(m))),
                    pl.BlockSpec((9, FFN_CW), lambda i, m, f=cv: (0, f(m))),
                    pl.BlockSpec((9, FFN_CW), lambda i, m, f=cv: (0, nj + f(m))),
                    pl.BlockSpec((1, FFN_CW), lambda i, m, f=cv: (0, f(m))),
                    pl.BlockSpec((1, FFN_CW), lambda i, m, f=cv: (0, nj + f(m))),
                    pl.BlockSpec((FFN_CW, D_MODEL), lambda i, m, f=cv: (f(m), 0))]
        w_args += [wup, wup, cw, cw, cb, cb, wd]
    return pl.pallas_call(
        functools.partial(_ffn_kernel, tiles_per_img, nj),
        out_shape=jax.ShapeDtypeStruct((rows, D_MODEL), F32),
        grid=(rows // tm, steps),
        in_specs=[pl.BlockSpec((tm, D_MODEL), lambda i, m: (i, 0)),
                  pl.BlockSpec((GRID_W, D_MODEL), lambda i, m: (jnp.maximum(i * hb - 1, 0), 0)),
                  pl.BlockSpec((GRID_W, D_MODEL), lambda i, m: (jnp.minimum((i + 1) * hb, nhb - 1), 0))]
                 + w_specs
                 + [pl.BlockSpec((tm, D_MODEL), lambda i, m: (i, 0)),
                    pl.BlockSpec((1, 1, D_MODEL), lambda i, m: (i // tiles_per_img, 0, 0)),
                    pl.BlockSpec((1, D_MODEL), lambda i, m: (0, 0))],
        out_specs=pl.BlockSpec((tm, D_MODEL), lambda i, m: (i, 0)),
        scratch_shapes=[pltpu.VMEM((ext, D_MODEL), BF16),
                        pltpu.VMEM((2, ext, FFN_CW), F32),
                        pltpu.VMEM((2, ext, FFN_CW), F32),
                        pltpu.VMEM((tm, D_MODEL), F32)],
        compiler_params=_cparams(("arbitrary", "arbitrary")),
        name="conv_ffn",
    )(h2, h2, h2, *w_args, lat1, gate2, fg)


def _dt_perm():
    idx = np.arange(2 * SSD_HEADS).reshape(2, SSD_GROUPS, SSD_HPG)
    return idx.transpose(1, 0, 2).reshape(-1)


def _expansion_matrices():
    mats = []
    for d in range(2):
        e = np.zeros((LANES, SSD_INNER), np.float32)
        for g in range(SSD_GROUPS):
            for h in range(SSD_HPG):
                c0 = g * SSD_GW + h * SSD_HEAD_DIM
                e[(2 * g + d) * SSD_HPG + h, c0:c0 + SSD_HEAD_DIM] = 1.0
        e[2 * SSD_HEADS:] = e[:2 * SSD_HEADS]
        mats.append(jnp.asarray(e).astype(BF16))
    return mats


def kernel(x, c, ctx, c_ctx, w_mod, b_mod, norm1_g, w_in, conv_ssd_w, conv_ssd_b, dt_bias, a_log,
           d_skip, ssd_norm_g, w_fa, w_sb, w_o, norm2_g, w_up, conv_ffn_w, conv_ffn_b, w_down, final_g):
    assert w_mod.shape[0] == 1, "single-layer block"
    nb, length, _ = x.shape
    clen = ctx.shape[1]
    assert length % FFN_TM == 0 and clen % SSD_CHUNK == 0

    cond = jnp.zeros((8, D_MODEL), F32).at[:nb].set(c).at[nb].set(c_ctx)
    mods = _adaln(cond, w_mod[0], b_mod[0][None, :])
    m_lat = [mods[:nb, k * D_MODEL:(k + 1) * D_MODEL][:, None, :] for k in range(N_MOD)]
    m_ctx = [mods[nb:nb + 1, k * D_MODEL:(k + 1) * D_MODEL][:, None, :] for k in range(2)]

    w = w_in[0].astype(BF16)
    o_f, o_xbc, o_z = D_MODEL, D_MODEL + XBC_WIDTH, D_MODEL + XBC_WIDTH + SSD_INNER
    o_dt = o_z + 2 * SSD_HEADS
    perm = _dt_perm()
    w_dt = jnp.pad(w[:, o_z:o_dt][:, perm], ((0, 0), (0, LANES - 2 * SSD_HEADS)))
    w_lat = jnp.concatenate([w[:, :o_z], w[:, o_dt:], w_dt], axis=1)
    w_ctx = jnp.concatenate([w[:, o_f:o_xbc], w_dt], axis=1)
    dtb = jnp.pad(dt_bias[0].reshape(-1)[perm], (0, LANES - 2 * SSD_HEADS))[None, :]
    g1 = norm1_g[0][None, :]

    x2 = x.reshape(nb * length, D_MODEL)
    cw, cb = conv_ssd_w[0], conv_ssd_b[0][None, :]
    segs_lat = (("id", D_MODEL), ("conv_silu", XBC_WIDTH), ("silu", SSD_INNER), ("sigmoid", D_MODEL),
                ("sigmoid", D_MODEL), ("softplus", LANES))
    u_f, xbc_l, zs, sgf, sgs, dt_l = _inproj(
        x2, m_lat[0], m_lat[1], g1, w_lat, dtb, cw, cb, segs_lat, (BF16, BF16, BF16, BF16, BF16, F32),
        length)
    segs_ctx = (("conv_silu", XBC_WIDTH), ("softplus", LANES))
    xbc_c, dt_c = _inproj(ctx.reshape(nb * clen, D_MODEL), m_ctx[0], m_ctx[1], g1, w_ctx, dtb, cw, cb,
                          segs_ctx, (BF16, F32), clen)

    alog = jnp.pad(a_log[0].reshape(-1)[perm], (0, LANES - 2 * SSD_HEADS))[None, :]
    dsk = jnp.repeat(d_skip[0], SSD_HEAD_DIM)[None, :]
    exp_f, exp_b = _expansion_matrices()
    zeros = jnp.zeros((nb, SSD_GROUPS, SSD_STATE, SSD_GW), F32)
    _, s_f, s_b = _ssd_scan(xbc_c, dt_c, alog, dsk, exp_f, exp_b, zeros, zeros,
                            nb, clen)
    y, _, _ = _ssd_scan(xbc_l, dt_l, alog, dsk, exp_f, exp_b, s_f, s_b,
                        nb, length)

    mixed = _fnet(u_f, nb, length)

    lat1, h2 = _merge(x2, mixed, y, zs, sgf, sgs, w_fa[0].astype(BF16), w_sb[0].astype(BF16),
                      w_o[0].astype(BF16), ssd_norm_g[0][None, :], m_lat[2], norm2_g[0][None, :],
                      m_lat[3], m_lat[4], length)

    out = _conv_ffn(h2, w_up[0].astype(BF16), conv_ffn_w[0].reshape(9, 2 * D_FF), conv_ffn_b[0][None, :],
                    w_down[0].astype(BF16), lat1, m_lat[5], final_g[None, :], length)
    return out.reshape(nb, length, D_MODEL)
```
